```python
import jax, jax.numpy as jnp
from jax import lax
import numpy as np

D_MODEL = 1024
BATCH = 2
SEQ = 8192
DEPTH = 2
DEC_BATCH = 128
DEC_SEQ = 8
PAST_LEN = 2048
PAGE_SIZE = 128

N_A = DEPTH // 2
N_B = DEPTH - N_A
D_CONV = D_MODEL
CONV_W = 3
N_HEADS = 16
HEAD_DIM = D_MODEL // N_HEADS
N_KV_HEADS = 4
GROUP = N_HEADS // N_KV_HEADS
CMP_BLK = 32
CMP_STRIDE = 16
CMP_RATIO = CMP_BLK // CMP_STRIDE
CMP_HID = HEAD_DIM
SEL_BLK = 64
TOP_N = 16
WINDOW = 512
Q_BLK = 64
D_FF = ((8 * D_MODEL // 3 + 255) // 256) * 256
RMS_EPS = 1e-6
BIG = 1e9
NEG = -1e30

kernel_name = 'yoco_shortconv_nsa_decode_step'


def rmsnorm(x, w):
    xf = x.astype(jnp.float32)
    y = xf * lax.rsqrt(jnp.mean(xf * xf, axis=-1, keepdims=True) + RMS_EPS)
    return (y * w.astype(jnp.float32)).astype(x.dtype)


def swiglu(x, w_in, w_out):
    g, u = jnp.split(x @ w_in, 2, axis=-1)
    return (jax.nn.silu(g) * u) @ w_out


def short_conv_mixer(xn, w_in, conv_w, w_out, prefix):
    b, c, u = jnp.split(xn @ w_in, 3, axis=-1)
    v = c * u
    vp = jnp.concatenate([prefix.astype(v.dtype), v], axis=1)
    T = v.shape[1]
    conv = conv_w[0] * vp[:, 0:T]
    for j in range(1, CONV_W):
        conv = conv + conv_w[j] * vp[:, j:j + T]
    return (b * conv) @ w_out, vp[:, -(CONV_W - 1):]


def masked_softmax(s, mask):
    s = jnp.where(mask, s, NEG)
    m = jnp.max(s, axis=-1, keepdims=True)
    e = jnp.where(mask, jnp.exp(s - m), 0.0)
    return e / jnp.maximum(jnp.sum(e, axis=-1, keepdims=True), 1e-30)


def shared_kv(h, kv_norm_w, kv_w):
    N, T, _ = h.shape
    return (rmsnorm(h, kv_norm_w) @ kv_w).reshape(N, T, 3, 2, N_KV_HEADS, HEAD_DIM)


def compress(kv, cmp_pe, cmp_w1, cmp_w2):
    N, L = kv.shape[:2]
    kv = jnp.pad(kv, ((0, 0), (0, (-L) % CMP_STRIDE), (0, 0), (0, 0), (0, 0)))
    C = kv.shape[1] // CMP_STRIDE
    NC = C - CMP_RATIO + 1
    chunks = kv.reshape(N, C, CMP_STRIDE, 2, N_KV_HEADS, HEAD_DIM)
    w1 = cmp_w1.reshape(2, CMP_RATIO, CMP_STRIDE, HEAD_DIM, CMP_HID)
    part = jnp.einsum('ncjkgd,krjdh->rnckgh', chunks, w1)
    pre = part[0, :, 0:NC]
    for r in range(1, CMP_RATIO):
        pre = pre + part[r, :, r:r + NC]
    pe_bias = jnp.einsum('kjd,kjdh->kh', cmp_pe, cmp_w1)
    hid = jax.nn.silu(pre + pe_bias[:, None, :])
    out = jnp.einsum('nckgh,khe->nckge', hid, cmp_w2)
    ends = jnp.arange(NC) * CMP_STRIDE + CMP_BLK - 1
    return out, ends


def sel_blocks(kv):
    N, L = kv.shape[:2]
    kv = jnp.pad(kv, ((0, 0), (0, (-L) % SEL_BLK), (0, 0), (0, 0), (0, 0)))
    kv = kv.reshape(N, -1, SEL_BLK, 2, N_KV_HEADS, HEAD_DIM)
    return jnp.moveaxis(kv, 4, 1)


def query_side(xn, w_in):
    N, T, _ = xn.shape
    proj = xn @ w_in
    q = proj[..., :N_HEADS * HEAD_DIM].reshape(N, T, N_KV_HEADS, GROUP, HEAD_DIM) * (HEAD_DIM ** -0.5)
    g = jax.nn.sigmoid(proj[..., N_HEADS * HEAD_DIM:].astype(jnp.float32)).astype(xn.dtype)
    return q, g.reshape(N, T, N_KV_HEADS, GROUP, 3)


def nsa_block(q, g, q_pos, kvc, cmp_end, kvs, kvw, kw_pos):
    N, T = q.shape[:2]
    s_c = jnp.einsum('ntgrd,ncgd->ntgrc', q, kvc[:, :, 0]).astype(jnp.float32)
    mask_c = (cmp_end[None, :] <= q_pos[:, None])[None, :, None, None, :]
    p_c = masked_softmax(s_c, mask_c)
    o_c = jnp.einsum('ntgrc,ncgd->ntgrd', p_c.astype(kvc.dtype), kvc[:, :, 1])
    NC = kvc.shape[1]
    NSEL = kvs.shape[2]
    c_start = jnp.arange(NC) * CMP_STRIDE
    s_start = jnp.arange(NSEL) * SEL_BLK
    overlap = jnp.clip(jnp.minimum(c_start[:, None] + CMP_BLK, s_start[None, :] + SEL_BLK)
                       - jnp.maximum(c_start[:, None], s_start[None, :]), 0, None)
    agg = overlap.astype(jnp.float32) / CMP_STRIDE
    imp = jnp.einsum('ntgrc,cj->ntgj', p_c, agg)
    valid = s_start[None, :] <= q_pos[:, None]
    cur = (q_pos // SEL_BLK)[:, None]
    j = jnp.arange(NSEL)[None, :]
    forced = (j == 0) | (j == cur) | (j == cur - 1)
    score = jnp.where((valid & forced)[None, :, None, :], BIG,
                      jnp.where(valid[None, :, None, :], imp, -BIG))
    _, idx = lax.top_k(score, min(TOP_N, NSEL))
    K = idx.shape[-1]
    n_ix = jnp.arange(N)[:, None, None, None]
    g_ix = jnp.arange(N_KV_HEADS)[None, None, :, None]
    sel = kvs[n_ix, g_ix, idx]
    sel = sel.reshape(N, T, N_KV_HEADS, K * SEL_BLK, 2, HEAD_DIM)
    tok = (idx[..., None] * SEL_BLK + jnp.arange(SEL_BLK)).reshape(N, T, N_KV_HEADS, K * SEL_BLK)
    mask_s = (tok <= q_pos[None, :, None, None])[:, :, :, None, :]
    s_s = jnp.einsum('ntgrd,ntgkd->ntgrk', q, sel[..., 0, :]).astype(jnp.float32)
    p_s = masked_softmax(s_s, mask_s)
    o_s = jnp.einsum('ntgrk,ntgkd->ntgrd', p_s.astype(sel.dtype), sel[..., 1, :])
    s_w = jnp.einsum('ntgrd,nkgd->ntgrk', q, kvw[:, :, 0]).astype(jnp.float32)
    mask_w = ((kw_pos[None, :] <= q_pos[:, None]) & (kw_pos[None, :] >= q_pos[:, None] - WINDOW)
              & (kw_pos[None, :] >= 0))[None, :, None, None, :]
    p_w = masked_softmax(s_w, mask_w)
    o_w = jnp.einsum('ntgrk,nkgd->ntgrd', p_w.astype(kvw.dtype), kvw[:, :, 1])
    return g[..., 0:1] * o_c + g[..., 1:2] * o_s + g[..., 2:3] * o_w


def nsa_prompt(xn, w_in, w_out, ctx):
    kvc, cmp_end, kvs, kvw = ctx
    N, T, _ = xn.shape
    q, g = query_side(xn, w_in)
    nb = T // Q_BLK
    kvw_pad = jnp.pad(kvw, ((0, 0), (WINDOW, 0), (0, 0), (0, 0), (0, 0)))

    def block(args):
        qb, gb, start = args
        q_pos = start + jnp.arange(Q_BLK)
        kw = lax.dynamic_slice_in_dim(kvw_pad, start, WINDOW + Q_BLK, axis=1)
        kw_pos = start - WINDOW + jnp.arange(WINDOW + Q_BLK)
        return nsa_block(qb, gb, q_pos, kvc, cmp_end, kvs, kw, kw_pos)

    def to_blocks(a):
        return a.reshape(N, nb, Q_BLK, *a.shape[2:]).swapaxes(0, 1)

    o = lax.map(block, (to_blocks(q), to_blocks(g), jnp.arange(nb) * Q_BLK))
    o = o.swapaxes(0, 1).reshape(N, T, N_HEADS * HEAD_DIM)
    return o @ w_out


def nsa_sample(xn, w_in, w_out, ctx):
    kvc, cmp_end, kvs, kvw, kw_pos = ctx
    N, T, _ = xn.shape
    q, g = query_side(xn, w_in)

    def step(args):
        qt, gt, pos = args
        return nsa_block(qt[:, None], gt[:, None], pos[None], kvc, cmp_end, kvs, kvw, kw_pos)[:, 0]

    o = lax.map(step, (q.swapaxes(0, 1), g.swapaxes(0, 1), PAST_LEN + jnp.arange(T)))
    o = o.swapaxes(0, 1).reshape(N, T, N_HEADS * HEAD_DIM)
    return o @ w_out


def prompt_kv_side(h, kv_norm_w, kv_w, cmp_pe, cmp_w1, cmp_w2):
    kv = shared_kv(h, kv_norm_w, kv_w)
    kvc, ends = compress(kv[:, :, 0], cmp_pe, cmp_w1, cmp_w2)
    kvs = sel_blocks(kv[:, :, 1])
    kvw = kv[:, :, 2]
    keep = min(WINDOW, h.shape[1])
    return (kvc, ends, kvs, kvw), (kv[:, :, 0], kv[:, :, 1], kvw[:, -keep:])


def sample_kv_side(h, cache_cmp_kv, cache_slc_kv, state_win_kv, page_table, kv_norm_w, kv_w, cmp_pe, cmp_w1, cmp_w2):
    N, T, _ = h.shape
    kv = shared_kv(h, kv_norm_w, kv_w)

    def past(pool):
        return pool[page_table].reshape(N, -1, 2, N_KV_HEADS, HEAD_DIM)

    full_cmp = jnp.concatenate([past(cache_cmp_kv), kv[:, :, 0]], axis=1)
    full_slc = jnp.concatenate([past(cache_slc_kv), kv[:, :, 1]], axis=1)
    kvc, ends = compress(full_cmp, cmp_pe, cmp_w1, cmp_w2)
    kvs = sel_blocks(full_slc)
    win = jnp.concatenate([state_win_kv, kv[:, :, 2]], axis=1)
    w_len = state_win_kv.shape[1]
    kw_pos = PAST_LEN - w_len + jnp.arange(w_len + T)
    keep = min(WINDOW, PAST_LEN + T)
    return (kvc, ends, kvs, win, kw_pos), (kv[:, :, 0], kv[:, :, 1], win[:, -keep:])


def trunk(x, conv_prefix, kv_side, nsa_mixer, norm_w, final_norm_w, a_in_w, a_conv_w, a_out_w,
          b_in_w, b_out_w, ffn_in_w, ffn_out_w):
    h = x
    conv_states = []
    ctx, kv_rows = None, None
    for l in range(DEPTH):
        xn = rmsnorm(h, norm_w[l, 0])
        if l < N_A:
            y, st = short_conv_mixer(xn, a_in_w[l], a_conv_w[l], a_out_w[l], conv_prefix[l])
            conv_states.append(st)
        else:
            if l == N_A:
                ctx, kv_rows = kv_side(h)
            y = nsa_mixer(xn, b_in_w[l - N_A], b_out_w[l - N_A], ctx)
        h = h + y
        h = h + swiglu(rmsnorm(h, norm_w[l, 1]), ffn_in_w[l], ffn_out_w[l])
    return rmsnorm(h, final_norm_w), jnp.stack(conv_states), kv_rows


def setup_inputs(seed: int = 0) -> dict:
    key = jax.random.key(seed)
    ks = jax.random.split(key, 24)
    f32 = jnp.float32

    def nrm(k, shape, scale):
        return jax.random.normal(k, shape, f32) * scale

    n_pages = PAST_LEN // PAGE_SIZE
    n_used = DEC_BATCH * n_pages
    n_pool = n_used + max(1, n_used // 4)
    page_table = jax.random.permutation(ks[0], n_pool)[:n_used].reshape(DEC_BATCH, n_pages).astype(jnp.int32)
    kv_row = (2, N_KV_HEADS, HEAD_DIM)
    qg = N_HEADS * HEAD_DIM + 3 * N_HEADS
    return {
        'x_prompt': nrm(ks[1], (BATCH, SEQ, D_MODEL), 1.0),
        'x_sample': nrm(ks[2], (DEC_BATCH, DEC_SEQ, D_MODEL), 1.0),
        'cache_cmp_kv': nrm(ks[3], (n_pool, PAGE_SIZE) + kv_row, 1.0),
        'cache_slc_kv': nrm(ks[4], (n_pool, PAGE_SIZE) + kv_row, 1.0),
        'state_win_kv': nrm(ks[5], (DEC_BATCH, min(WINDOW, PAST_LEN)) + kv_row, 1.0),
        'state_conv': nrm(ks[6], (N_A, DEC_BATCH, CONV_W - 1, D_CONV), 1.0),
        'page_table': page_table,
        'norm_w': 1.0 + nrm(ks[7], (DEPTH, 2, D_MODEL), 0.02),
        'final_norm_w': 1.0 + nrm(ks[8], (D_MODEL,), 0.02),
        'a_in_w': nrm(ks[9], (N_A, D_MODEL, 3 * D_CONV), D_MODEL ** -0.5),
        'a_conv_w': nrm(ks[10], (N_A, CONV_W, D_CONV), CONV_W ** -0.5),
        'a_out_w': nrm(ks[11], (N_A, D_CONV, D_MODEL), D_CONV ** -0.5),
        'b_in_w': nrm(ks[12], (N_B, D_MODEL, qg), D_MODEL ** -0.5),
        'b_out_w': nrm(ks[13], (N_B, N_HEADS * HEAD_DIM, D_MODEL), (N_HEADS * HEAD_DIM) ** -0.5),
        'kv_norm_w': 1.0 + nrm(ks[14], (D_MODEL,), 0.02),
        'kv_w': nrm(ks[15], (D_MODEL, 3 * 2 * N_KV_HEADS * HEAD_DIM), D_MODEL ** -0.5),
        'cmp_pe': nrm(ks[16], (2, CMP_BLK, HEAD_DIM), 0.1),
        'cmp_w1': nrm(ks[17], (2, CMP_BLK, HEAD_DIM, CMP_HID), (CMP_BLK * HEAD_DIM) ** -0.5),
        'cmp_w2': nrm(ks[18], (2, CMP_HID, HEAD_DIM), CMP_HID ** -0.5),
        'ffn_in_w': nrm(ks[19], (DEPTH, D_MODEL, 2 * D_FF), D_MODEL ** -0.5),
        'ffn_out_w': nrm(ks[20], (DEPTH, D_FF, D_MODEL), D_FF ** -0.5),
    }


def reference(x_prompt, x_sample, cache_cmp_kv, cache_slc_kv, state_win_kv, state_conv, page_table,
              norm_w, final_norm_w, a_in_w, a_conv_w, a_out_w, b_in_w, b_out_w, kv_norm_w, kv_w,
              cmp_pe, cmp_w1, cmp_w2, ffn_in_w, ffn_out_w):
    conv_zero = jnp.zeros((N_A, x_prompt.shape[0], CONV_W - 1, D_CONV), x_prompt.dtype)
    y_prompt, conv_p, kv_rows_p = trunk(
        x_prompt, conv_zero,
        lambda h: prompt_kv_side(h, kv_norm_w, kv_w, cmp_pe, cmp_w1, cmp_w2),
        nsa_prompt, norm_w, final_norm_w, a_in_w, a_conv_w, a_out_w, b_in_w, b_out_w, ffn_in_w, ffn_out_w)
    y_sample, conv_s, kv_rows_s = trunk(
        x_sample, state_conv,
        lambda h: sample_kv_side(h, cache_cmp_kv, cache_slc_kv, state_win_kv, page_table,
                                 kv_norm_w, kv_w, cmp_pe, cmp_w1, cmp_w2),
        nsa_sample, norm_w, final_norm_w, a_in_w, a_conv_w, a_out_w, b_in_w, b_out_w, ffn_in_w, ffn_out_w)
    cmp_p, slc_p, win_p = kv_rows_p
    cmp_s, slc_s, win_s = kv_rows_s
    return (y_prompt, y_sample, conv_p, conv_s, cmp_p, cmp_s, slc_p, slc_s, win_p, win_s)
```

```python
import functools

import jax
import jax.numpy as jnp
from jax import lax
from jax.experimental import pallas as pl
from jax.experimental.pallas import tpu as pltpu

F32 = jnp.float32
BF16 = jnp.bfloat16

HEAD_DIM = 64
N_HEADS = 16
N_KV_HEADS = 4
GROUP = N_HEADS // N_KV_HEADS
CMP_BLK = 32
CMP_STRIDE = 16
SEL_BLK = 64
TOP_N = 16
WINDOW = 512
PAGE_SIZE = 128
CONV_W = 3
RMS_EPS = 1e-6
BIG = 1e9
NEG = -1e30

LANES = 128
SUBLANES = 8
KV_ROW = 2 * N_KV_HEADS * HEAD_DIM
CHUNK_LANES = CMP_STRIDE * KV_ROW
VMEM_LIMIT_BYTES = 56 * 1024 * 1024

_NT = (((1,), (1,)), ((), ()))


def _dot(a, b):
    return jnp.dot(a, b, preferred_element_type=F32)


def _dot_nt(a, b):
    return lax.dot_general(a, b, _NT, preferred_element_type=F32)


def _rms_unit(x):
    return x * lax.rsqrt(jnp.mean(x * x, axis=-1, keepdims=True) + RMS_EPS)


def _silu(x):
    return x * jax.nn.sigmoid(x)


def _params(*sem):
    return pltpu.CompilerParams(dimension_semantics=sem, vmem_limit_bytes=VMEM_LIMIT_BYTES)


def _conv_gate(xn_bf16, win_ref, cw_ref, vm1_fix, vm2_fix, d):
    b = _dot(xn_bf16, win_ref[:, 0:d])
    c = _dot(xn_bf16, win_ref[:, d:2 * d])
    u = _dot(xn_bf16, win_ref[:, 2 * d:3 * d])
    v = c * u
    vm1 = vm1_fix(pltpu.roll(v, 1, axis=0))
    vm2 = vm2_fix(pltpu.roll(v, 2, axis=0))
    cw = cw_ref[...]
    conv = cw[0:1, :] * vm2 + cw[1:2, :] * vm1 + cw[2:3, :] * v
    return b * conv, v


def _l0_in_prompt_kernel(x_ref, nw_ref, win_ref, cw_ref, z_ref, tail_ref, carry_ref):
    i = pl.program_id(1)
    tm, d = x_ref.shape

    @pl.when(i == 0)
    def _():
        carry_ref[...] = jnp.zeros_like(carry_ref)

    xn = (_rms_unit(x_ref[...]) * nw_ref[...]).astype(BF16)
    prev = carry_ref[...]
    row = lax.broadcasted_iota(jnp.int32, (tm, 1), 0)
    fix1 = lambda r: jnp.where(row == 0, prev[7:8, :], r)
    fix2 = lambda r: jnp.where(row == 0, prev[6:7, :], jnp.where(row == 1, prev[7:8, :], r))
    z, v = _conv_gate(xn, win_ref, cw_ref, fix1, fix2, d)
    z_ref[...] = z.astype(z_ref.dtype)
    carry_ref[...] = v[tm - SUBLANES:tm, :]

    @pl.when(i == pl.num_programs(1) - 1)
    def _():
        tail_ref[...] = v[tm - SUBLANES:tm, :]


def _l0_in_sample_kernel(x_ref, p1_ref, p2_ref, nw_ref, win_ref, cw_ref, z_ref, v_ref, *, seq):
    tm, d = x_ref.shape
    xn = (_rms_unit(x_ref[...]) * nw_ref[...]).astype(BF16)
    t = lax.broadcasted_iota(jnp.int32, (tm, 1), 0) & (seq - 1)
    fix1 = lambda r: jnp.where(t == 0, p1_ref[...], r)
    fix2 = lambda r: jnp.where(t < 2, p2_ref[...], r)
    z, v = _conv_gate(xn, win_ref, cw_ref, fix1, fix2, d)
    z_ref[...] = z.astype(z_ref.dtype)
    v_ref[...] = v


def _l0_in_prompt(x, nw, w_in, conv_w, tm):
    n, t, d = x.shape
    return pl.pallas_call(
        _l0_in_prompt_kernel,
        grid=(n, t // tm),
        in_specs=[
            pl.BlockSpec((None, tm, d), lambda b, i: (b, i, 0)),
            pl.BlockSpec((1, d), lambda b, i: (0, 0)),
            pl.BlockSpec((d, 3 * d), lambda b, i: (0, 0)),
            pl.BlockSpec((CONV_W, d), lambda b, i: (0, 0)),
        ],
        out_specs=[
            pl.BlockSpec((None, tm, d), lambda b, i: (b, i, 0)),
            pl.BlockSpec((None, SUBLANES, d), lambda b, i: (b, 0, 0)),
        ],
        out_shape=[jax.ShapeDtypeStruct((n, t, d), BF16), jax.ShapeDtypeStruct((n, SUBLANES, d), F32)],
        scratch_shapes=[pltpu.VMEM((SUBLANES, d), F32)],
        compiler_params=_params("arbitrary", "arbitrary"),
        name="l0_in_prompt",
    )(x, nw, w_in, conv_w)


def _l0_in_sample(x, pre1, pre2, nw, w_in, conv_w, seq, tm):
    m, d = x.shape
    row = pl.BlockSpec((tm, d), lambda i: (i, 0))
    return pl.pallas_call(
        functools.partial(_l0_in_sample_kernel, seq=seq),
        grid=(m // tm,),
        in_specs=[row, row, row,
                  pl.BlockSpec((1, d), lambda i: (0, 0)),
                  pl.BlockSpec((d, 3 * d), lambda i: (0, 0)),
                  pl.BlockSpec((CONV_W, d), lambda i: (0, 0))],
        out_specs=[row, row],
        out_shape=[jax.ShapeDtypeStruct((m, d), BF16), jax.ShapeDtypeStruct((m, d), F32)],
        compiler_params=_params("arbitrary"),
        name="l0_in_sample",
    )(x, pre1, pre2, nw, w_in, conv_w)


def _mix_ffn_kernel(h_ref, z_ref, wo_ref, nw_ref, wg_ref, wu_ref, wd_ref, fnw_ref, o_ref, hn_ref, *, final):
    j = pl.program_id(1)

    @pl.when(j == 0)
    def _():
        h1 = h_ref[...] + _dot(z_ref[...].astype(BF16), wo_ref[...])
        o_ref[...] = h1
        hn_ref[...] = (_rms_unit(h1) * nw_ref[...]).astype(BF16)

    hn = hn_ref[...]
    act = _silu(_dot(hn, wg_ref[...])) * _dot(hn, wu_ref[...])
    o_ref[...] += _dot(act.astype(BF16), wd_ref[...])

    if final:
        @pl.when(j == pl.num_programs(1) - 1)
        def _():
            o_ref[...] = _rms_unit(o_ref[...]) * fnw_ref[...]


def _mix_ffn(h, z, wo, nw, w_in, w_out, fnw, final, tm, tf):
    m, d = h.shape
    kz = z.shape[1]
    f = w_out.shape[0]
    nf = f // tf
    return pl.pallas_call(
        functools.partial(_mix_ffn_kernel, final=final),
        grid=(m // tm, nf),
        in_specs=[
            pl.BlockSpec((tm, d), lambda i, j: (i, 0)),
            pl.BlockSpec((tm, kz), lambda i, j: (i, 0)),
            pl.BlockSpec((kz, d), lambda i, j: (0, 0)),
            pl.BlockSpec((1, d), lambda i, j: (0, 0)),
            pl.BlockSpec((d, tf), lambda i, j: (0, j)),
            pl.BlockSpec((d, tf), lambda i, j: (0, j + nf)),
            pl.BlockSpec((tf, d), lambda i, j: (j, 0)),
            pl.BlockSpec((1, d), lambda i, j: (0, 0)),
        ],
        out_specs=pl.BlockSpec((tm, d), lambda i, j: (i, 0)),
        out_shape=jax.ShapeDtypeStruct((m, d), F32),
        scratch_shapes=[pltpu.VMEM((tm, d), BF16)],
        compiler_params=_params("arbitrary", "arbitrary"),
        name="mix_ffn_final" if final else "mix_ffn",
    )(h, z, wo, nw, w_in, w_in, w_out, fnw)


def _l1_in_kernel(h_ref, nw_ref, kvnw_ref, wq_ref, wg_ref, wkv_ref,
                  q_ref, g_ref, cmp_ref, slc_ref, win_ref, kvb_ref):
    unit = _rms_unit(h_ref[...])
    xn = (unit * nw_ref[...]).astype(BF16)
    kn = (unit * kvnw_ref[...]).astype(BF16)
    q_ref[...] = (_dot(xn, wq_ref[...]) * (HEAD_DIM ** -0.5)).astype(q_ref.dtype)
    g_ref[...] = jax.nn.sigmoid(_dot(xn, wg_ref[...]))
    kv = _dot(kn, wkv_ref[...])
    cmp_ref[...] = kv[:, 0:KV_ROW]
    slc_ref[...] = kv[:, KV_ROW:2 * KV_ROW]
    win_ref[...] = kv[:, 2 * KV_ROW:3 * KV_ROW]
    kvb_ref[...] = kv.astype(BF16)


def _l1_in(h, nw, kvnw, wq, wg, wkv, q_dtype, tm):
    m, d = h.shape
    nq, ng = wq.shape[1], wg.shape[1]
    row = lambda w: pl.BlockSpec((tm, w), lambda i: (i, 0))
    full = lambda a: pl.BlockSpec(a.shape, lambda i: (0, 0))
    return pl.pallas_call(
        _l1_in_kernel,
        grid=(m // tm,),
        in_specs=[row(d), full(nw), full(kvnw), full(wq), full(wg), full(wkv)],
        out_specs=[row(nq), row(ng), row(KV_ROW), row(KV_ROW), row(KV_ROW), row(3 * KV_ROW)],
        out_shape=[jax.ShapeDtypeStruct((m, nq), q_dtype), jax.ShapeDtypeStruct((m, ng), F32),
                   jax.ShapeDtypeStruct((m, KV_ROW), F32), jax.ShapeDtypeStruct((m, KV_ROW), F32),
                   jax.ShapeDtypeStruct((m, KV_ROW), F32), jax.ShapeDtypeStruct((m, 3 * KV_ROW), BF16)],
        compiler_params=_params("arbitrary"),
        name="l1_in",
    )(h, nw, kvnw, wq, wg, wkv)


def _compress_tokens(x_ref, n_rows, tc, pe_ref, w1d_ref, w1p_ref, w2_ref, shift_ref):
    pieces = []
    for p in range(4):
        k = p // 2
        xp = jnp.concatenate(
            [x_ref[0:n_rows, (j * 4 + p) * LANES:(j * 4 + p + 1) * LANES] for j in range(CMP_STRIDE)],
            axis=1).astype(BF16)
        part = _dot(xp, w1p_ref[k])
        shift_ref[0:n_rows, :] = part[:, LANES:2 * LANES]
        peb = _dot(pe_ref[k].astype(BF16), w1d_ref[k])
        pre = part[0:tc, 0:LANES] + shift_ref[pl.ds(1, tc), :] + peb
        pieces.append(_dot(_silu(pre).astype(BF16), w2_ref[k]))
    return jnp.concatenate(pieces, axis=1)


def _compress_prompt_kernel(x_ref, halo_ref, pe_ref, w1d_ref, w1p_ref, w2_ref, kc_ref, kct_ref, xs_ref, shift_ref):
    tc = x_ref.shape[0]
    xs_ref[0:tc, :] = x_ref[...]
    xs_ref[tc:tc + SUBLANES, :] = halo_ref[...]
    kc = _compress_tokens(xs_ref, tc + SUBLANES, tc, pe_ref, w1d_ref, w1p_ref, w2_ref, shift_ref)
    kc_ref[...] = kc.astype(kc_ref.dtype)
    kct_ref[...] = kc.T.astype(kct_ref.dtype)


def _compress_prompt(chunks, pe, w1d, w1p, w2, tc):
    n, c, _ = chunks.shape
    nhalo = c // SUBLANES
    full = lambda a: pl.BlockSpec(a.shape, lambda b, i: (0,) * a.ndim)
    return pl.pallas_call(
        _compress_prompt_kernel,
        grid=(n, c // tc),
        in_specs=[
            pl.BlockSpec((None, tc, CHUNK_LANES), lambda b, i: (b, i, 0)),
            pl.BlockSpec((None, SUBLANES, CHUNK_LANES),
                         lambda b, i: (b, jnp.minimum((i + 1) * (tc // SUBLANES), nhalo - 1), 0)),
            full(pe), full(w1d), full(w1p), full(w2),
        ],
        out_specs=[pl.BlockSpec((None, tc, KV_ROW), lambda b, i: (b, i, 0)),
                   pl.BlockSpec((None, KV_ROW, tc), lambda b, i: (b, 0, i))],
        out_shape=[jax.ShapeDtypeStruct((n, c, KV_ROW), BF16), jax.ShapeDtypeStruct((n, KV_ROW, c), BF16)],
        scratch_shapes=[pltpu.VMEM((tc + SUBLANES, CHUNK_LANES), F32), pltpu.VMEM((tc + SUBLANES, LANES), F32)],
        compiler_params=_params("arbitrary", "arbitrary"),
        name="compress_prompt",
    )(chunks, chunks, pe, w1d, w1p, w2)


def _compress_sample_kernel(pt_ref, new_ref, pe_ref, w1d_ref, w1p_ref, w2_ref, *rest, n_pages):
    page_refs = rest[:n_pages]
    kc_ref, xs_ref, shift_ref = rest[n_pages:]
    per = page_refs[0].shape[0]
    tc = n_pages * per
    for p in range(n_pages):
        xs_ref[p * per:(p + 1) * per, :] = page_refs[p][...]
    xs_ref[tc:tc + SUBLANES, :] = jnp.zeros((SUBLANES, CHUNK_LANES), F32)
    xs_ref[tc:tc + 1, :] = new_ref[...]
    kc = _compress_tokens(xs_ref, tc + SUBLANES, tc, pe_ref, w1d_ref, w1p_ref, w2_ref, shift_ref)
    kc_ref[...] = kc.astype(kc_ref.dtype)


def _compress_sample(page_table, cache_chunks, new_chunk, pe, w1d, w1p, w2):
    nb = new_chunk.shape[0]
    n_pages = page_table.shape[0] // nb
    per = cache_chunks.shape[1]
    tc = n_pages * per
    full = lambda a: pl.BlockSpec(a.shape, lambda b, pt: (0,) * a.ndim)
    page_specs = [pl.BlockSpec((None, per, CHUNK_LANES),
                               lambda b, pt, p=p: (pt[b * n_pages + p], 0, 0)) for p in range(n_pages)]
    return pl.pallas_call(
        functools.partial(_compress_sample_kernel, n_pages=n_pages),
        grid_spec=pltpu.PrefetchScalarGridSpec(
            num_scalar_prefetch=1,
            grid=(nb,),
            in_specs=[pl.BlockSpec((None, 1, CHUNK_LANES), lambda b, pt: (b, 0, 0)),
                      full(pe), full(w1d), full(w1p), full(w2)] + page_specs,
            out_specs=pl.BlockSpec((None, tc, KV_ROW), lambda b, pt: (b, 0, 0)),
            scratch_shapes=[pltpu.VMEM((tc + SUBLANES, CHUNK_LANES), F32),
                            pltpu.VMEM((tc + SUBLANES, LANES), F32)],
        ),
        out_shape=jax.ShapeDtypeStruct((nb, tc, KV_ROW), BF16),
        compiler_params=_params("arbitrary"),
        name="compress_sample",
    )(page_table, new_chunk, pe, w1d, w1p, w2, *([cache_chunks] * n_pages))


def _top_n_rows(score, n_pick):
    rows = lax.broadcasted_iota(jnp.int32, score.shape, 0).astype(F32)
    sel = jnp.zeros(score.shape, F32)
    s = score
    for _ in range(n_pick):
        m = jnp.max(s, axis=0, keepdims=True)
        first = jnp.min(jnp.where(s == m, rows, float(score.shape[0])), axis=0, keepdims=True)
        hit = rows == first
        sel = jnp.where(hit, 1.0, sel)
        s = jnp.where(hit, -jnp.inf, s)
    return sel


def _softmax_cols(s, mask):
    s = jnp.where(mask, s, NEG)
    m = jnp.max(s, axis=0, keepdims=True)
    e = jnp.where(mask, jnp.exp(s - m), 0.0)
    return e * (1.0 / jnp.maximum(jnp.sum(e, axis=0, keepdims=True), 1e-30))


def _softmax_rows_unnorm(s, mask):
    s = jnp.where(mask, s, NEG)
    m = jnp.max(s, axis=1, keepdims=True)
    e = jnp.where(mask, jnp.exp(s - m), 0.0)
    return e, 1.0 / jnp.maximum(jnp.sum(e, axis=1, keepdims=True), 1e-30)


def _block_scores(imp, pos, n_sel):
    jj = lax.broadcasted_iota(jnp.int32, imp.shape, 0)
    valid = (jj * SEL_BLK <= pos) & (jj < n_sel)
    cur = pos // SEL_BLK
    forced = (jj == 0) | (jj == cur) | (jj == cur - 1)
    score = jnp.where(valid & forced, BIG, jnp.where(valid, imp, -BIG))
    return jnp.where(jj < n_sel, score, -jnp.inf)


def _split_dot(w_bf16, x):
    hi = x.astype(BF16)
    lo = (x - hi.astype(F32)).astype(BF16)
    return _dot(w_bf16, hi) + _dot(w_bf16, lo)


def _attn_prompt_kernel(q_ref, g_ref, kc_ref, vct_ref, sk_ref, sv_ref, wk_ref, wv_ref, e_ref, aggt_ref, z_ref,
                        *, tq, tkc):
    qt = pl.program_id(2)
    t0 = qt * tq
    r_rows = GROUP * tq
    n_cmp = kc_ref.shape[0]
    n_sel = e_ref.shape[0] // SEL_BLK
    half = lax.broadcasted_iota(jnp.int32, (tq, LANES), 1) // HEAD_DIM
    qpos_row = t0 + (lax.broadcasted_iota(jnp.int32, (1, r_rows), 1) & (tq - 1))
    qpos_col = t0 + (lax.broadcasted_iota(jnp.int32, (r_rows, 1), 0) & (tq - 1))
    gates = g_ref[...]

    for gg in range(2):
        qs = [q_ref[:, (gg * GROUP + r) * LANES:(gg * GROUP + r + 1) * LANES] for r in range(GROUP)]
        qall = jnp.concatenate(qs, axis=0)

        cend = lax.broadcasted_iota(jnp.int32, (n_cmp, 1), 0) * CMP_STRIDE + (CMP_BLK - 1)
        pt = _softmax_cols(_dot_nt(kc_ref[...], qall), cend <= qpos_row)
        psum = pt[:, 0:tq]
        for r in range(1, GROUP):
            psum = psum + pt[:, r * tq:(r + 1) * tq]
        imp = _split_dot(aggt_ref[...], psum)
        oct = _dot(vct_ref[...], pt.astype(BF16))

        pos = t0 + lax.broadcasted_iota(jnp.int32, imp.shape, 1)
        selt = _top_n_rows(_block_scores(imp, pos, n_sel), min(TOP_N, n_sel))
        selm1 = (selt.T - 1.0).astype(BF16)
        qaug = jnp.concatenate([jnp.concatenate([qs[r], selm1], axis=1) for r in range(GROUP)], axis=0)

        def sel_step(c, carry):
            m_i, l_i, acc = carry
            k0 = pl.multiple_of(c * tkc, tkc)
            rhs = jnp.concatenate([sk_ref[pl.ds(k0, tkc), :], e_ref[pl.ds(k0, tkc), :]], axis=1)
            s = _dot_nt(qaug, rhs)
            kpos = k0 + lax.broadcasted_iota(jnp.int32, (1, tkc), 1)
            s = jnp.where(kpos <= qpos_col, s, NEG)
            m_new = jnp.maximum(m_i, jnp.max(s, axis=1, keepdims=True))
            alpha = jnp.exp(m_i - m_new)
            p = jnp.exp(s - m_new)
            l_new = alpha * l_i + jnp.sum(p, axis=1, keepdims=True)
            acc = alpha * acc + _dot(p.astype(BF16), sv_ref[pl.ds(k0, tkc), :])
            return m_new, l_new, acc

        n_chunks = (t0 + tq + tkc - 1) // tkc
        init = (jnp.full((r_rows, 1), NEG, F32), jnp.zeros((r_rows, 1), F32), jnp.zeros((r_rows, LANES), F32))
        _, l_s, acc_s = lax.fori_loop(0, n_chunks, sel_step, init)
        o_s = acc_s * (1.0 / jnp.maximum(l_s, 1e-30))

        start = pl.multiple_of(jnp.maximum(t0 - WINDOW, 0), LANES)
        kpos = start + lax.broadcasted_iota(jnp.int32, (1, WINDOW + tq), 1)
        mw = (kpos <= qpos_col) & (kpos >= qpos_col - WINDOW)
        ew, inv_w = _softmax_rows_unnorm(_dot_nt(qall, wk_ref[pl.ds(start, WINDOW + tq), :]), mw)
        o_w = _dot(ew.astype(BF16), wv_ref[pl.ds(start, WINDOW + tq), :]) * inv_w

        for r in range(GROUP):
            c = (gg * GROUP + r) * 3
            rows = slice(r * tq, (r + 1) * tq)
            out = (gates[:, c:c + 1] * oct[:, rows].T + gates[:, c + 1:c + 2] * o_s[rows, :]
                   + gates[:, c + 2:c + 3] * o_w[rows, :])
            col = gg * GROUP + r
            z_ref[:, col * LANES:(col + 1) * LANES] = jnp.where(half == gg, out, 0.0).astype(z_ref.dtype)


def _attn_prompt(q2, gates2, kvc, kvct, kvb, e_mat, aggt, tq, tkc):
    n, t, _ = q2.shape
    npair = N_KV_HEADS // 2
    pair_w = 2 * GROUP * LANES
    kv_col = lambda branch, kv: pl.BlockSpec(
        (None, t, LANES), lambda b, gp, i: (b, 0, branch * (KV_ROW // LANES) + kv * npair + gp))
    full = lambda a: pl.BlockSpec(a.shape, lambda b, gp, i: (0,) * a.ndim)
    return pl.pallas_call(
        functools.partial(_attn_prompt_kernel, tq=tq, tkc=tkc),
        grid=(n, npair, t // tq),
        in_specs=[
            pl.BlockSpec((None, tq, pair_w), lambda b, gp, i: (b, i, gp)),
            pl.BlockSpec((None, tq, LANES), lambda b, gp, i: (b, i, gp)),
            pl.BlockSpec((None, kvc.shape[1], LANES), lambda b, gp, i: (b, 0, gp)),
            pl.BlockSpec((None, LANES, kvct.shape[2]), lambda b, gp, i: (b, npair + gp, 0)),
            kv_col(1, 0), kv_col(1, 1), kv_col(2, 0), kv_col(2, 1),
            full(e_mat), full(aggt),
        ],
        out_specs=pl.BlockSpec((None, tq, pair_w), lambda b, gp, i: (b, i, gp)),
        out_shape=jax.ShapeDtypeStruct((n, t, npair * pair_w), BF16),
        compiler_params=_params("arbitrary", "arbitrary", "arbitrary"),
        name="attn_prompt",
    )(q2, gates2, kvc, kvct, kvb, kvb, kvb, kvb, e_mat, aggt)


def _attn_sample_kernel(pt_ref, qbd_ref, g_ref, kvc_ref, nslc_ref, wst_ref, nwin_ref, e_ref, aggt_ref, *rest,
                        n_pages, past_len):
    page_refs = rest[:n_pages]
    z_ref, wout_ref = rest[n_pages:]
    seq = qbd_ref.shape[0]
    kw = N_KV_HEADS * HEAD_DIM
    rows = GROUP * N_KV_HEADS * seq
    w_len = wst_ref.shape[0]
    n_cmp = kvc_ref.shape[0]
    n_keys = n_pages * PAGE_SIZE + seq
    n_sel = -(-n_keys // SEL_BLK)

    qbd = jnp.concatenate(
        [qbd_ref[:, (g * GROUP + r) * kw:(g * GROUP + r + 1) * kw] for r in range(GROUP) for g in range(N_KV_HEADS)],
        axis=0).astype(BF16)
    qpos_row = past_len + (lax.broadcasted_iota(jnp.int32, (1, rows), 1) & (seq - 1))
    qpos_col = past_len + (lax.broadcasted_iota(jnp.int32, (rows, 1), 0) & (seq - 1))

    cend = lax.broadcasted_iota(jnp.int32, (n_cmp, 1), 0) * CMP_STRIDE + (CMP_BLK - 1)
    pt = _softmax_cols(_dot_nt(kvc_ref[:, 0:kw], qbd), cend <= qpos_row)
    imp = _split_dot(aggt_ref[...], pt)
    per_r = N_KV_HEADS * seq
    imp_g = imp
    for r in range(1, GROUP):
        imp_g = imp_g + pltpu.roll(imp, r * per_r, axis=1)
    vct = kvc_ref[:, kw:2 * kw].astype(F32).T.astype(BF16)
    o_c = _dot(vct, pt.astype(BF16)).T

    selt = _top_n_rows(_block_scores(imp_g, jnp.broadcast_to(qpos_row, imp_g.shape), n_sel), min(TOP_N, n_sel))
    selm1 = (selt.T - 1.0).astype(BF16)
    qaug = jnp.concatenate([qbd, selm1], axis=1)

    pad = jnp.zeros((PAGE_SIZE - seq, kw), F32)
    s_tiles = []
    for p in range(n_pages + 1):
        k_rows = page_refs[p][:, 0:kw] if p < n_pages else jnp.concatenate([nslc_ref[:, 0:kw], pad], axis=0)
        rhs = jnp.concatenate([k_rows.astype(BF16), e_ref[p * PAGE_SIZE:(p + 1) * PAGE_SIZE, :]], axis=1)
        s_tiles.append(_dot_nt(qaug, rhs))
    s_all = jnp.concatenate(s_tiles, axis=1)
    kpos = lax.broadcasted_iota(jnp.int32, (1, s_all.shape[1]), 1)
    es, inv_s = _softmax_rows_unnorm(s_all, kpos <= qpos_col)
    es = es.astype(BF16)
    o_s = jnp.zeros((rows, kw), F32)
    for p in range(n_pages + 1):
        v_rows = page_refs[p][:, kw:2 * kw] if p < n_pages else jnp.concatenate([nslc_ref[:, kw:2 * kw], pad], axis=0)
        o_s = o_s + _dot(es[:, p * PAGE_SIZE:(p + 1) * PAGE_SIZE], v_rows.astype(BF16))
    o_s = o_s * inv_s

    k_w = jnp.concatenate([wst_ref[:, 0:kw], nwin_ref[:, 0:kw], pad], axis=0).astype(BF16)
    v_w = jnp.concatenate([wst_ref[:, kw:2 * kw], nwin_ref[:, kw:2 * kw], pad], axis=0).astype(BF16)
    kpos_w = past_len - w_len + lax.broadcasted_iota(jnp.int32, (1, w_len + PAGE_SIZE), 1)
    mw = (kpos_w <= qpos_col) & (kpos_w >= qpos_col - WINDOW) & (kpos_w >= 0)
    ew, inv_w = _softmax_rows_unnorm(_dot_nt(qbd, k_w), mw)
    o_w = _dot(ew.astype(BF16), v_w) * inv_w

    gates = g_ref[...]
    half = lax.broadcasted_iota(jnp.int32, (seq, LANES), 1) // HEAD_DIM
    for r in range(GROUP):
        for g in range(N_KV_HEADS):
            head = g * GROUP + r
            rs = slice((r * N_KV_HEADS + g) * seq, (r * N_KV_HEADS + g + 1) * seq)
            cs = slice((g // 2) * LANES, (g // 2 + 1) * LANES)
            c = head * 3
            out = (gates[:, c:c + 1] * o_c[rs, cs] + gates[:, c + 1:c + 2] * o_s[rs, cs]
                   + gates[:, c + 2:c + 3] * o_w[rs, cs])
            z_ref[:, head * LANES:(head + 1) * LANES] = jnp.where(half == g % 2, out, 0.0)

    keep = wout_ref.shape[0]
    wout_ref[0:keep - seq, :] = wst_ref[w_len - (keep - seq):w_len, :]
    wout_ref[keep - seq:keep, :] = nwin_ref[...]


def _attn_sample(page_table, qbd, gates, kvc, new_slc, cache_slc, win_state, new_win, e_mat, aggt, past_len):
    nb, seq, _ = qbd.shape
    n_pages = page_table.shape[0] // nb
    w_len = win_state.shape[1]
    keep = min(WINDOW, past_len + seq)
    full = lambda a: pl.BlockSpec(a.shape, lambda b, pt: (0,) * a.ndim)
    per_b = lambda a: pl.BlockSpec((None,) + a.shape[1:], lambda b, pt: (b,) + (0,) * (a.ndim - 1))
    page_specs = [pl.BlockSpec((None, PAGE_SIZE, KV_ROW),
                               lambda b, pt, p=p: (pt[b * n_pages + p], 0, 0)) for p in range(n_pages)]
    zw = N_HEADS * LANES
    return pl.pallas_call(
        functools.partial(_attn_sample_kernel, n_pages=n_pages, past_len=past_len),
        grid_spec=pltpu.PrefetchScalarGridSpec(
            num_scalar_prefetch=1,
            grid=(nb,),
            in_specs=[per_b(qbd), per_b(gates), per_b(kvc), per_b(new_slc), per_b(win_state), per_b(new_win),
                      full(e_mat), full(aggt)] + page_specs,
            out_specs=[pl.BlockSpec((None, seq, zw), lambda b, pt: (b, 0, 0)),
                       pl.BlockSpec((None, keep, KV_ROW), lambda b, pt: (b, 0, 0))],
        ),
        out_shape=[jax.ShapeDtypeStruct((nb, seq, zw), F32), jax.ShapeDtypeStruct((nb, keep, KV_ROW), F32)],
        compiler_params=_params("arbitrary"),
        name="attn_sample",
    )(page_table, qbd, gates, kvc, new_slc, win_state, new_win, e_mat, aggt, *([cache_slc] * n_pages))


def _pad_cols(w, width):
    return jnp.pad(w, ((0, 0), (0, width - w.shape[1])))


def _pack_weights(b_in_w, b_out_w, cmp_w1, cmp_w2):
    d = b_in_w.shape[0]
    nq = N_HEADS * HEAD_DIM
    wq = b_in_w[:, :nq].reshape(d, N_KV_HEADS, GROUP, HEAD_DIM)
    wgate = b_in_w[:, nq:]
    eye2 = jnp.eye(2, dtype=F32)
    eye4 = jnp.eye(N_KV_HEADS, dtype=F32)
    par = jnp.arange(N_KV_HEADS) % 2
    sel2 = jax.nn.one_hot(par, 2, dtype=F32)
    wq_p = jnp.einsum('dgrh,ga->dgrah', wq, sel2).reshape(d, N_HEADS * LANES)
    wq_s = jnp.einsum('dgrh,ga->dgrah', wq, eye4).reshape(d, N_HEADS * N_KV_HEADS * HEAD_DIM)
    per_pair = 2 * GROUP * 3
    wg_p = jnp.concatenate([_pad_cols(wgate[:, i * per_pair:(i + 1) * per_pair], LANES)
                            for i in range(N_KV_HEADS // 2)], axis=1)
    wg_s = _pad_cols(wgate, LANES)
    wo = b_out_w.reshape(N_KV_HEADS, GROUP, HEAD_DIM, d)
    wo_pad = jnp.einsum('grhd,ga->grahd', wo, sel2).reshape(N_HEADS * LANES, d)
    w1 = cmp_w1.reshape(2, CMP_BLK // CMP_STRIDE, CMP_STRIDE, HEAD_DIM, -1)
    hid = w1.shape[-1]
    w1p = jnp.einsum('krjdh,ab->kjadrbh', w1, eye2).reshape(2, CMP_STRIDE * LANES, 2 * 2 * hid)
    w1d = jnp.concatenate([cmp_w1.reshape(2, CMP_BLK * HEAD_DIM, hid)] * 2, axis=2)
    w2p = jnp.einsum('khe,ab->kahbe', cmp_w2, eye2).reshape(2, 2 * hid, 2 * HEAD_DIM)
    cast = lambda a: a.astype(BF16)
    return tuple(map(cast, (wq_p, wq_s, wg_p, wg_s, wo_pad, w1p, w1d, w2p)))


def _agg_t(n_cmp, n_sel_pad):
    c0 = jnp.arange(n_cmp) * CMP_STRIDE
    s0 = jnp.arange(n_sel_pad) * SEL_BLK
    ov = jnp.clip(jnp.minimum(c0[None, :] + CMP_BLK, s0[:, None] + SEL_BLK) - jnp.maximum(c0[None, :], s0[:, None]),
                  0, None)
    return (ov.astype(F32) / CMP_STRIDE).astype(BF16)


def _block_onehot(n_keys):
    blk = jnp.arange(n_keys) // SEL_BLK
    return (jax.nn.one_hot(blk, LANES, dtype=F32) * BIG).astype(BF16)


def kernel(x_prompt, x_sample, cache_cmp_kv, cache_slc_kv, state_win_kv, state_conv, page_table, norm_w, final_norm_w, a_in_w, a_conv_w, a_out_w, b_in_w, b_out_w, kv_norm_w, kv_w, cmp_pe, cmp_w1, cmp_w2, ffn_in_w, ffn_out_w):
    n, t, d = x_prompt.shape
    nb, ts, _ = x_sample.shape
    n_pool = cache_cmp_kv.shape[0]
    n_pages = page_table.shape[1]
    past_len = n_pages * PAGE_SIZE
    d_ff = ffn_out_w.shape[1]
    assert a_in_w.shape[0] == 1 and b_in_w.shape[0] == 1 and d == N_HEADS * HEAD_DIM
    assert t % 1024 == 0 and ts == SUBLANES and (nb * ts) % 512 == 0 and d_ff % 256 == 0

    wq_p, wq_s, wg_p, wg_s, wo_pad, w1p, w1d, w2p = _pack_weights(b_in_w[0], b_out_w[0], cmp_w1, cmp_w2)
    a_in, a_out, kvw = a_in_w[0].astype(BF16), a_out_w[0].astype(BF16), kv_w.astype(BF16)
    ffn_in, ffn_out = ffn_in_w.astype(BF16), ffn_out_w.astype(BF16)
    nw = norm_w.reshape(norm_w.shape[0], 2, 1, d)
    fnw = final_norm_w.reshape(1, d)
    kvnw = kv_norm_w.reshape(1, d)
    pe = cmp_pe.reshape(2, 1, CMP_BLK * HEAD_DIM)
    tf = d_ff // 2
    mp, ms = n * t, nb * ts

    z0, tail_p = _l0_in_prompt(x_prompt, nw[0, 0], a_in, a_conv_w[0], tm=512)
    h1 = _mix_ffn(x_prompt.reshape(mp, d), z0.reshape(mp, d), a_out, nw[0, 1], ffn_in[0], ffn_out[0], fnw,
                  final=False, tm=512, tf=tf)
    q2, g2, cmp_p, slc_p, win_p, kvb = _l1_in(h1, nw[1, 0], kvnw, wq_p, wg_p, kvw, BF16, tm=512)
    n_chunks = t // CMP_STRIDE
    kvc, kvct = _compress_prompt(cmp_p.reshape(n, n_chunks, CHUNK_LANES), pe, w1d, w1p, w2p, tc=128)
    z1 = _attn_prompt(q2.reshape(n, t, -1), g2.reshape(n, t, -1), kvc, kvct, kvb.reshape(n, t, -1),
                      _block_onehot(t), _agg_t(n_chunks, LANES), tq=128, tkc=512)
    y_p = _mix_ffn(h1, z1.reshape(mp, -1), wo_pad, nw[1, 1], ffn_in[1], ffn_out[1], fnw,
                   final=True, tm=512, tf=tf)

    zero = jnp.zeros((nb, ts - 2, d), F32)
    pre1 = jnp.concatenate([state_conv[0, :, 1:2], jnp.zeros((nb, ts - 1, d), F32)], axis=1).reshape(ms, d)
    pre2 = jnp.concatenate([state_conv[0], zero], axis=1).reshape(ms, d)
    xs = x_sample.reshape(ms, d)
    z0s, v_s = _l0_in_sample(xs, pre1, pre2, nw[0, 0], a_in, a_conv_w[0], seq=ts, tm=512)
    h1s = _mix_ffn(xs, z0s, a_out, nw[0, 1], ffn_in[0], ffn_out[0], fnw, final=False, tm=512, tf=tf)
    qbd, gs, cmp_s, slc_s, win_s_new, _ = _l1_in(h1s, nw[1, 0], kvnw, wq_s, wg_s, kvw, F32, tm=512)
    pt_flat = page_table.reshape(-1)
    new_chunk = jnp.pad(cmp_s.reshape(nb, 1, ts * KV_ROW), ((0, 0), (0, 0), (0, CHUNK_LANES - ts * KV_ROW)))
    kvc_s = _compress_sample(pt_flat, cache_cmp_kv.reshape(n_pool, PAGE_SIZE // CMP_STRIDE, CHUNK_LANES),
                             new_chunk, pe, w1d, w1p, w2p)
    n_keys_pad = past_len + PAGE_SIZE
    z1s, win_s = _attn_sample(pt_flat, qbd.reshape(nb, ts, -1), gs.reshape(nb, ts, -1), kvc_s,
                              slc_s.reshape(nb, ts, KV_ROW), cache_slc_kv.reshape(n_pool, PAGE_SIZE, KV_ROW),
                              state_win_kv.reshape(nb, -1, KV_ROW), win_s_new.reshape(nb, ts, KV_ROW),
                              _block_onehot(n_keys_pad), _agg_t(kvc_s.shape[1], LANES), past_len)
    y_s = _mix_ffn(h1s, z1s.reshape(ms, -1), wo_pad, nw[1, 1], ffn_in[1], ffn_out[1], fnw,
                   final=True, tm=512, tf=tf)

    kv_shape = (2, N_KV_HEADS, HEAD_DIM)
    keep_p = min(WINDOW, t)
    return (
        y_p.reshape(n, t, d),
        y_s.reshape(nb, ts, d),
        tail_p[:, SUBLANES - (CONV_W - 1):][None],
        v_s.reshape(nb, ts, d)[:, ts - (CONV_W - 1):][None],
        cmp_p.reshape((n, t) + kv_shape),
        cmp_s.reshape((nb, ts) + kv_shape),
        slc_p.reshape((n, t) + kv_shape),
        slc_s.reshape((nb, ts) + kv_shape),
        win_p.reshape((n, t) + kv_shape)[:, t - keep_p:],
        win_s.reshape((nb, -1) + kv_shape),
    )
```

```python
import functools

import jax
import jax.numpy as jnp
from jax import lax
from jax.experimental import pallas as pl
from jax.experimental.pallas import tpu as pltpu

F32 = jnp.float32
BF16 = jnp.bfloat16

HEAD_DIM = 64
N_HEADS = 16
N_KV_HEADS = 4
GROUP = N_HEADS // N_KV_HEADS
CMP_BLK = 32
CMP_STRIDE = 16
SEL_BLK = 64
TOP_N = 16
WINDOW = 512
PAGE_SIZE = 128
CONV_W = 3
RMS_EPS = 1e-6
BIG = 1e9
NEG = -1e30

LANES = 128
SUBLANES = 8
KV_ROW = 2 * N_KV_HEADS * HEAD_DIM
KV_HALF = KV_ROW // 2
CHUNK_LANES = CMP_STRIDE * KV_ROW
VMEM_LIMIT_BYTES = 56 * 1024 * 1024

ROW_TILE = 512
CMP_TILE = 128
Q_TILE = 128
KEY_CHUNK = 512

_NT = (((1,), (1,)), ((), ()))


def _dot(a, b):
    return jnp.dot(a, b, preferred_element_type=F32)


def _dot_nt(a, b):
    return lax.dot_general(a, b, _NT, preferred_element_type=F32)


def _rms_unit(x):
    return x * lax.rsqrt(jnp.mean(x * x, axis=-1, keepdims=True) + RMS_EPS)


def _silu(x):
    return x * jax.nn.sigmoid(x)


def _params(*sem):
    return pltpu.CompilerParams(dimension_semantics=sem, vmem_limit_bytes=VMEM_LIMIT_BYTES)


def _conv_gate(xn_bf16, win_ref, cw_ref, vm1_fix, vm2_fix, d):
    b = _dot(xn_bf16, win_ref[:, 0:d])
    c = _dot(xn_bf16, win_ref[:, d:2 * d])
    u = _dot(xn_bf16, win_ref[:, 2 * d:3 * d])
    v = c * u
    vm1 = vm1_fix(pltpu.roll(v, 1, axis=0))
    vm2 = vm2_fix(pltpu.roll(v, 2, axis=0))
    cw = cw_ref[...]
    conv = cw[0:1, :] * vm2 + cw[1:2, :] * vm1 + cw[2:3, :] * v
    return b * conv, v


def _l0_in_prompt_kernel(x_ref, nw_ref, win_ref, cw_ref, z_ref, tail_ref, carry_ref):
    i = pl.program_id(1)
    tm, d = x_ref.shape

    @pl.when(i == 0)
    def _():
        carry_ref[...] = jnp.zeros_like(carry_ref)

    xn = (_rms_unit(x_ref[...]) * nw_ref[...]).astype(BF16)
    prev = carry_ref[...]
    row = lax.broadcasted_iota(jnp.int32, (tm, 1), 0)
    fix1 = lambda r: jnp.where(row == 0, prev[7:8, :], r)
    fix2 = lambda r: jnp.where(row == 0, prev[6:7, :], jnp.where(row == 1, prev[7:8, :], r))
    z, v = _conv_gate(xn, win_ref, cw_ref, fix1, fix2, d)
    z_ref[...] = z.astype(z_ref.dtype)
    carry_ref[...] = v[tm - SUBLANES:tm, :]

    @pl.when(i == pl.num_programs(1) - 1)
    def _():
        tail_ref[...] = v[tm - SUBLANES:tm, :]


def _l0_in_sample_kernel(x_ref, p1_ref, p2_ref, nw_ref, win_ref, cw_ref, z_ref, v_ref, *, seq):
    tm, d = x_ref.shape
    xn = (_rms_unit(x_ref[...]) * nw_ref[...]).astype(BF16)
    t = lax.broadcasted_iota(jnp.int32, (tm, 1), 0) & (seq - 1)
    fix1 = lambda r: jnp.where(t == 0, p1_ref[...], r)
    fix2 = lambda r: jnp.where(t < 2, p2_ref[...], r)
    z, v = _conv_gate(xn, win_ref, cw_ref, fix1, fix2, d)
    z_ref[...] = z.astype(z_ref.dtype)
    v_ref[...] = v


def _l0_in_prompt(x, nw, w_in, conv_w):
    n, t, d = x.shape
    tm = ROW_TILE
    return pl.pallas_call(
        _l0_in_prompt_kernel,
        grid=(n, t // tm),
        in_specs=[
            pl.BlockSpec((None, tm, d), lambda b, i: (b, i, 0)),
            pl.BlockSpec((1, d), lambda b, i: (0, 0)),
            pl.BlockSpec((d, 3 * d), lambda b, i: (0, 0)),
            pl.BlockSpec((CONV_W, d), lambda b, i: (0, 0)),
        ],
        out_specs=[
            pl.BlockSpec((None, tm, d), lambda b, i: (b, i, 0)),
            pl.BlockSpec((None, SUBLANES, d), lambda b, i: (b, 0, 0)),
        ],
        out_shape=[jax.ShapeDtypeStruct((n, t, d), BF16), jax.ShapeDtypeStruct((n, SUBLANES, d), F32)],
        scratch_shapes=[pltpu.VMEM((SUBLANES, d), F32)],
        compiler_params=_params("arbitrary", "arbitrary"),
        name="l0_in_prompt",
    )(x, nw, w_in, conv_w)


def _l0_in_sample(x, pre1, pre2, nw, w_in, conv_w, seq):
    m, d = x.shape
    tm = ROW_TILE
    row = pl.BlockSpec((tm, d), lambda i: (i, 0))
    return pl.pallas_call(
        functools.partial(_l0_in_sample_kernel, seq=seq),
        grid=(m // tm,),
        in_specs=[row, row, row,
                  pl.BlockSpec((1, d), lambda i: (0, 0)),
                  pl.BlockSpec((d, 3 * d), lambda i: (0, 0)),
                  pl.BlockSpec((CONV_W, d), lambda i: (0, 0))],
        out_specs=[row, row],
        out_shape=[jax.ShapeDtypeStruct((m, d), BF16), jax.ShapeDtypeStruct((m, d), F32)],
        compiler_params=_params("arbitrary"),
        name="l0_in_sample",
    )(x, pre1, pre2, nw, w_in, conv_w)


def _mix_ffn_kernel(h_ref, z_ref, wo_ref, nw_ref, wg_ref, wu_ref, wd_ref, fnw_ref, o_ref, hn_ref, *, final):
    j = pl.program_id(1)

    @pl.when(j == 0)
    def _():
        h1 = h_ref[...] + _dot(z_ref[...].astype(BF16), wo_ref[...])
        o_ref[...] = h1
        hn_ref[...] = (_rms_unit(h1) * nw_ref[...]).astype(BF16)

    hn = hn_ref[...]
    act = _silu(_dot(hn, wg_ref[...])) * _dot(hn, wu_ref[...])
    o_ref[...] += _dot(act.astype(BF16), wd_ref[...])

    if final:
        @pl.when(j == pl.num_programs(1) - 1)
        def _():
            o_ref[...] = _rms_unit(o_ref[...]) * fnw_ref[...]


def _mix_ffn(h, z, wo, nw, w_in, w_out, fnw, final):
    m, d = h.shape
    kz = z.shape[1]
    f = w_out.shape[0]
    tm, tf = ROW_TILE, f // 2
    nf = f // tf
    return pl.pallas_call(
        functools.partial(_mix_ffn_kernel, final=final),
        grid=(m // tm, nf),
        in_specs=[
            pl.BlockSpec((tm, d), lambda i, j: (i, 0)),
            pl.BlockSpec((tm, kz), lambda i, j: (i, 0)),
            pl.BlockSpec((kz, d), lambda i, j: (0, 0)),
            pl.BlockSpec((1, d), lambda i, j: (0, 0)),
            pl.BlockSpec((d, tf), lambda i, j: (0, j)),
            pl.BlockSpec((d, tf), lambda i, j: (0, j + nf)),
            pl.BlockSpec((tf, d), lambda i, j: (j, 0)),
            pl.BlockSpec((1, d), lambda i, j: (0, 0)),
        ],
        out_specs=pl.BlockSpec((tm, d), lambda i, j: (i, 0)),
        out_shape=jax.ShapeDtypeStruct((m, d), F32),
        scratch_shapes=[pltpu.VMEM((tm, d), BF16)],
        compiler_params=_params("arbitrary", "arbitrary"),
        name="mix_ffn_final" if final else "mix_ffn",
    )(h, z, wo, nw, w_in, w_in, w_out, fnw)


def _l1_proj(h_ref, nw_ref, kvnw_ref, wq_ref, wg_ref, wkv_ref, q_ref, g_ref):
    unit = _rms_unit(h_ref[...])
    xn = (unit * nw_ref[...]).astype(BF16)
    kn = (unit * kvnw_ref[...]).astype(BF16)
    q_ref[...] = (_dot(xn, wq_ref[...]) * (HEAD_DIM ** -0.5)).astype(q_ref.dtype)
    g_ref[...] = jax.nn.sigmoid(_dot(xn, wg_ref[...]))
    return _dot(kn, wkv_ref[...])


def _l1_in_prompt_kernel(h_ref, nw_ref, kvnw_ref, wq_ref, wg_ref, wkv_ref,
                         q_ref, g_ref, cmpr_ref, cmpt_ref, slct_ref, wint_ref, kvtb_ref):
    kv = _l1_proj(h_ref, nw_ref, kvnw_ref, wq_ref, wg_ref, wkv_ref, q_ref, g_ref)
    cmpr_ref[...] = kv[:, 0:KV_ROW]
    kvt = kv.T
    cmpt_ref[...] = kvt[0:KV_ROW, :]
    slct_ref[...] = kvt[KV_ROW:2 * KV_ROW, :]
    wint_ref[...] = kvt[2 * KV_ROW:3 * KV_ROW, :]
    kvtb_ref[...] = kvt[KV_ROW:3 * KV_ROW, :].astype(BF16)


def _l1_in_sample_kernel(h_ref, nw_ref, kvnw_ref, wq_ref, wg_ref, wkv_ref, q_ref, g_ref, cmp_ref, slc_ref, win_ref):
    kv = _l1_proj(h_ref, nw_ref, kvnw_ref, wq_ref, wg_ref, wkv_ref, q_ref, g_ref)
    cmp_ref[...] = kv[:, 0:KV_ROW]
    slc_ref[...] = kv[:, KV_ROW:2 * KV_ROW]
    win_ref[...] = kv[:, 2 * KV_ROW:3 * KV_ROW]


def _l1_in_prompt(h, nw, kvnw, wq, wg, wkv):
    n, t, d = h.shape
    tm = ROW_TILE
    nq, ng = wq.shape[1], wg.shape[1]
    row = lambda w: pl.BlockSpec((None, tm, w), lambda b, i: (b, i, 0))
    col = lambda r: pl.BlockSpec((None, r, tm), lambda b, i: (b, 0, i))
    full = lambda a: pl.BlockSpec(a.shape, lambda b, i: (0, 0))
    tshape = lambda r, dt: jax.ShapeDtypeStruct((n, r, t), dt)
    return pl.pallas_call(
        _l1_in_prompt_kernel,
        grid=(n, t // tm),
        in_specs=[row(d), full(nw), full(kvnw), full(wq), full(wg), full(wkv)],
        out_specs=[row(nq), row(ng), row(KV_ROW), col(KV_ROW), col(KV_ROW), col(KV_ROW), col(2 * KV_ROW)],
        out_shape=[jax.ShapeDtypeStruct((n, t, nq), BF16), jax.ShapeDtypeStruct((n, t, ng), F32),
                   jax.ShapeDtypeStruct((n, t, KV_ROW), F32),
                   tshape(KV_ROW, F32), tshape(KV_ROW, F32), tshape(KV_ROW, F32), tshape(2 * KV_ROW, BF16)],
        compiler_params=_params("arbitrary", "arbitrary"),
        name="l1_in_prompt",
    )(h, nw, kvnw, wq, wg, wkv)


def _l1_in_sample(h, nw, kvnw, wq, wg, wkv):
    m, d = h.shape
    tm = ROW_TILE
    nq, ng = wq.shape[1], wg.shape[1]
    row = lambda w: pl.BlockSpec((tm, w), lambda i: (i, 0))
    full = lambda a: pl.BlockSpec(a.shape, lambda i: (0, 0))
    return pl.pallas_call(
        _l1_in_sample_kernel,
        grid=(m // tm,),
        in_specs=[row(d), full(nw), full(kvnw), full(wq), full(wg), full(wkv)],
        out_specs=[row(nq), row(ng), row(KV_ROW), row(KV_ROW), row(KV_ROW)],
        out_shape=[jax.ShapeDtypeStruct((m, nq), F32), jax.ShapeDtypeStruct((m, ng), F32),
                   jax.ShapeDtypeStruct((m, KV_ROW), F32), jax.ShapeDtypeStruct((m, KV_ROW), F32),
                   jax.ShapeDtypeStruct((m, KV_ROW), F32)],
        compiler_params=_params("arbitrary"),
        name="l1_in_sample",
    )(h, nw, kvnw, wq, wg, wkv)


def _compress_tokens(gather, tc, pe_ref, w1d_ref, w1p_ref, w2_ref, shift_ref):
    pieces = []
    for p in range(KV_ROW // LANES):
        k = p // 2
        xp = jnp.concatenate([gather(p, j) for j in range(CMP_STRIDE)], axis=1).astype(BF16)
        part = _dot(xp, w1p_ref[k])
        shift_ref[...] = part[:, LANES:2 * LANES]
        peb = _dot(pe_ref[k].astype(BF16), w1d_ref[k])
        pre = part[0:tc, 0:LANES] + shift_ref[pl.ds(1, tc), :] + peb
        pieces.append(_dot(_silu(pre).astype(BF16), w2_ref[k]))
    return jnp.concatenate(pieces, axis=1)


def _compress_prompt_kernel(x_ref, halo_ref, pe_ref, w1d_ref, w1p_ref, w2_ref, kc_ref, kct_ref, xs_ref, shift_ref):
    tc = x_ref.shape[0]
    xs_ref[0:tc, :] = x_ref[...]
    xs_ref[tc:tc + SUBLANES, :] = halo_ref[...]
    gather = lambda p, j: xs_ref[:, (j * 4 + p) * LANES:(j * 4 + p + 1) * LANES]
    kc = _compress_tokens(gather, tc, pe_ref, w1d_ref, w1p_ref, w2_ref, shift_ref)
    kc_ref[...] = kc.astype(kc_ref.dtype)
    kct_ref[...] = kc.T.astype(kct_ref.dtype)


def _compress_prompt(chunks, pe, w1d, w1p, w2):
    n, c, _ = chunks.shape
    tc = CMP_TILE
    nhalo = c // SUBLANES
    full = lambda a: pl.BlockSpec(a.shape, lambda b, i: (0,) * a.ndim)
    return pl.pallas_call(
        _compress_prompt_kernel,
        grid=(n, c // tc),
        in_specs=[
            pl.BlockSpec((None, tc, CHUNK_LANES), lambda b, i: (b, i, 0)),
            pl.BlockSpec((None, SUBLANES, CHUNK_LANES),
                         lambda b, i: (b, jnp.minimum((i + 1) * (tc // SUBLANES), nhalo - 1), 0)),
            full(pe), full(w1d), full(w1p), full(w2),
        ],
        out_specs=[pl.BlockSpec((None, tc, KV_ROW), lambda b, i: (b, i, 0)),
                   pl.BlockSpec((None, KV_ROW, tc), lambda b, i: (b, 0, i))],
        out_shape=[jax.ShapeDtypeStruct((n, c, KV_ROW), BF16), jax.ShapeDtypeStruct((n, KV_ROW, c), BF16)],
        scratch_shapes=[pltpu.VMEM((tc + SUBLANES, CHUNK_LANES), F32), pltpu.VMEM((tc + SUBLANES, LANES), F32)],
        compiler_params=_params("arbitrary", "arbitrary"),
        name="compress_prompt",
    )(chunks, chunks, pe, w1d, w1p, w2)


def _compress_sample_kernel(pt_ref, new_ref, pe_ref, w1d_ref, w1p_ref, w2_ref, *rest, n_pages):
    page_refs = rest[:n_pages]
    kc_ref, xs_ref, shift_ref = rest[n_pages:]
    n_pieces = KV_ROW // LANES
    past = n_pages * PAGE_SIZE
    seq = new_ref.shape[0]
    tc = past // CMP_STRIDE
    rows = xs_ref.shape[1]
    for pg in range(n_pages):
        page = page_refs[pg][...]
        for p in range(n_pieces):
            xs_ref[p, pg * PAGE_SIZE:(pg + 1) * PAGE_SIZE, :] = page[p * LANES:(p + 1) * LANES, :].T
    for p in range(n_pieces):
        xs_ref[p, past:past + seq, :] = new_ref[:, p * LANES:(p + 1) * LANES]
        xs_ref[p, past + seq:rows, :] = jnp.zeros((rows - past - seq, LANES), F32)
    gather = lambda p, j: xs_ref[p, pl.ds(j, tc + SUBLANES, stride=CMP_STRIDE), :]
    kc = _compress_tokens(gather, tc, pe_ref, w1d_ref, w1p_ref, w2_ref, shift_ref)
    kc_ref[...] = kc.astype(kc_ref.dtype)


def _compress_sample(page_table, cache_t, new_rows, pe, w1d, w1p, w2):
    nb, seq, _ = new_rows.shape
    n_pages = page_table.shape[0] // nb
    tc = n_pages * PAGE_SIZE // CMP_STRIDE
    full = lambda a: pl.BlockSpec(a.shape, lambda b, pt: (0,) * a.ndim)
    page_specs = [pl.BlockSpec((None, KV_ROW, PAGE_SIZE),
                               lambda b, pt, p=p: (pt[b * n_pages + p], 0, 0)) for p in range(n_pages)]
    return pl.pallas_call(
        functools.partial(_compress_sample_kernel, n_pages=n_pages),
        grid_spec=pltpu.PrefetchScalarGridSpec(
            num_scalar_prefetch=1,
            grid=(nb,),
            in_specs=[pl.BlockSpec((None, seq, KV_ROW), lambda b, pt: (b, 0, 0)),
                      full(pe), full(w1d), full(w1p), full(w2)] + page_specs,
            out_specs=pl.BlockSpec((None, tc, KV_ROW), lambda b, pt: (b, 0, 0)),
            scratch_shapes=[pltpu.VMEM((KV_ROW // LANES, (tc + SUBLANES) * CMP_STRIDE, LANES), F32),
                            pltpu.VMEM((tc + SUBLANES, LANES), F32)],
        ),
        out_shape=jax.ShapeDtypeStruct((nb, tc, KV_ROW), BF16),
        compiler_params=_params("arbitrary"),
        name="compress_sample",
    )(page_table, new_rows, pe, w1d, w1p, w2, *([cache_t] * n_pages))


def _top_n_rows(score, n_pick):
    rows = lax.broadcasted_iota(jnp.int32, score.shape, 0).astype(F32)
    sel = jnp.zeros(score.shape, F32)
    s = score
    for _ in range(n_pick):
        m = jnp.max(s, axis=0, keepdims=True)
        first = jnp.min(jnp.where(s == m, rows, float(score.shape[0])), axis=0, keepdims=True)
        hit = rows == first
        sel = jnp.where(hit, 1.0, sel)
        s = jnp.where(hit, -jnp.inf, s)
    return sel


def _softmax_cols(s, mask):
    s = jnp.where(mask, s, NEG)
    m = jnp.max(s, axis=0, keepdims=True)
    e = jnp.where(mask, jnp.exp(s - m), 0.0)
    return e * (1.0 / jnp.maximum(jnp.sum(e, axis=0, keepdims=True), 1e-30))


def _softmax_rows_unnorm(s, mask):
    s = jnp.where(mask, s, NEG)
    m = jnp.max(s, axis=1, keepdims=True)
    e = jnp.where(mask, jnp.exp(s - m), 0.0)
    return e, 1.0 / jnp.maximum(jnp.sum(e, axis=1, keepdims=True), 1e-30)


def _block_scores(imp, pos, n_sel):
    jj = lax.broadcasted_iota(jnp.int32, imp.shape, 0)
    valid = (jj * SEL_BLK <= pos) & (jj < n_sel)
    cur = pos // SEL_BLK
    forced = (jj == 0) | (jj == cur) | (jj == cur - 1)
    score = jnp.where(valid & forced, BIG, jnp.where(valid, imp, -BIG))
    return jnp.where(jj < n_sel, score, -jnp.inf)


def _split_dot(w_bf16, x):
    hi = x.astype(BF16)
    lo = (x - hi.astype(F32)).astype(BF16)
    return _dot(w_bf16, hi) + _dot(w_bf16, lo)


def _with_ones(v_half, ones, first):
    return jnp.concatenate([v_half, ones] if first else [ones, v_half], axis=0)


def _norm_by_other_half(acc):
    return acc * (1.0 / jnp.maximum(pltpu.roll(acc, HEAD_DIM, axis=1), 1e-30))


def _attn_prompt_kernel(q_ref, g_ref, kc_ref, vct_ref, skt_ref, svt_ref, wkt_ref, wvt_ref, et_ref, aggt_ref, z_ref):
    tq = q_ref.shape[0]
    tkc = KEY_CHUNK
    qt = pl.program_id(2)
    t0 = qt * tq
    heads = 2 * GROUP
    r_rows = heads * tq
    n_cmp = kc_ref.shape[0]
    n_sel = et_ref.shape[1] // SEL_BLK
    half = lax.broadcasted_iota(jnp.int32, (tq, LANES), 1) // HEAD_DIM
    qpos_row = t0 + (lax.broadcasted_iota(jnp.int32, (1, r_rows), 1) & (tq - 1))
    qpos_col = t0 + (lax.broadcasted_iota(jnp.int32, (r_rows, 1), 0) & (tq - 1))
    ones = jnp.ones((HEAD_DIM, tkc), BF16)

    qs = [q_ref[:, h * LANES:(h + 1) * LANES] for h in range(heads)]
    qall = jnp.concatenate(qs, axis=0)

    cend = lax.broadcasted_iota(jnp.int32, (n_cmp, 1), 0) * CMP_STRIDE + (CMP_BLK - 1)
    pt = _softmax_cols(_dot_nt(kc_ref[...], qall), cend <= qpos_row)
    psum = []
    for gg in range(2):
        acc = pt[:, gg * GROUP * tq:(gg * GROUP + 1) * tq]
        for r in range(1, GROUP):
            acc = acc + pt[:, (gg * GROUP + r) * tq:(gg * GROUP + r + 1) * tq]
        psum.append(acc)
    imp = _split_dot(aggt_ref[...], jnp.concatenate(psum, axis=1))
    oct = _dot(vct_ref[...], pt.astype(BF16))

    pos = t0 + (lax.broadcasted_iota(jnp.int32, imp.shape, 1) & (tq - 1))
    selt = _top_n_rows(_block_scores(imp, pos, n_sel), min(TOP_N, n_sel))
    selm1 = [(selt[:, gg * tq:(gg + 1) * tq].T - 1.0).astype(BF16) for gg in range(2)]
    qaug = jnp.concatenate([jnp.concatenate([qs[h], selm1[h // GROUP]], axis=1) for h in range(heads)], axis=0)

    def sel_chunk(k0, carry, causal):
        m_i, acc = carry
        rhs = jnp.concatenate([skt_ref[:, pl.ds(k0, tkc)], et_ref[:, pl.ds(k0, tkc)]], axis=0)
        s = _dot(qaug, rhs)
        if causal:
            kpos = k0 + lax.broadcasted_iota(jnp.int32, (1, tkc), 1)
            s = jnp.where(kpos <= qpos_col, s, NEG)
        m_new = jnp.maximum(m_i, jnp.max(s, axis=1, keepdims=True))
        alpha = jnp.exp(m_i - m_new)
        p = jnp.exp(s - m_new).astype(BF16)
        v = svt_ref[:, pl.ds(k0, tkc)]
        pv = jnp.concatenate(
            [_dot_nt(p[gg * GROUP * tq:(gg + 1) * GROUP * tq],
                     _with_ones(v[gg * HEAD_DIM:(gg + 1) * HEAD_DIM], ones, gg == 0)) for gg in range(2)], axis=0)
        return m_new, alpha * acc + pv

    n_full = t0 // tkc
    init = (jnp.full((r_rows, 1), NEG, F32), jnp.zeros((r_rows, LANES), F32))
    carry = lax.fori_loop(0, n_full, lambda c, cr: sel_chunk(pl.multiple_of(c * tkc, tkc), cr, False), init)
    _, acc_s = sel_chunk(pl.multiple_of(n_full * tkc, tkc), carry, True)
    o_s = _norm_by_other_half(acc_s)

    wlen = WINDOW + tq
    start = pl.multiple_of(jnp.maximum(t0 - WINDOW, 0), LANES)
    kpos = start + lax.broadcasted_iota(jnp.int32, (1, wlen), 1)
    mw = (kpos <= qpos_col) & (kpos >= qpos_col - WINDOW)
    sw = jnp.where(mw, _dot(qall, wkt_ref[:, pl.ds(start, wlen)]), NEG)
    ew = jnp.where(mw, jnp.exp(sw - jnp.max(sw, axis=1, keepdims=True)), 0.0).astype(BF16)
    vw = wvt_ref[:, pl.ds(start, wlen)]
    ones_w = jnp.ones((HEAD_DIM, wlen), BF16)
    o_w = _norm_by_other_half(jnp.concatenate(
        [_dot_nt(ew[gg * GROUP * tq:(gg + 1) * GROUP * tq],
                 _with_ones(vw[gg * HEAD_DIM:(gg + 1) * HEAD_DIM], ones_w, gg == 0)) for gg in range(2)], axis=0))

    gates = g_ref[...]
    for h in range(heads):
        c = h * 3
        rows = slice(h * tq, (h + 1) * tq)
        out = (gates[:, c:c + 1] * oct[:, rows].T + gates[:, c + 1:c + 2] * o_s[rows, :]
               + gates[:, c + 2:c + 3] * o_w[rows, :])
        z_ref[:, h * LANES:(h + 1) * LANES] = jnp.where(half == h // GROUP, out, 0.0).astype(z_ref.dtype)


def _attn_prompt(q2, gates2, kvc, kvct, kvtb, e_t, aggt):
    n, t, _ = q2.shape
    tq = Q_TILE
    npair = N_KV_HEADS // 2
    pair_w = 2 * GROUP * LANES
    kv_slab = lambda branch, kv: pl.BlockSpec(
        (None, LANES, t), lambda b, gp, i: (b, branch * 2 * npair + kv * npair + gp, 0))
    full = lambda a: pl.BlockSpec(a.shape, lambda b, gp, i: (0,) * a.ndim)
    return pl.pallas_call(
        _attn_prompt_kernel,
        grid=(n, npair, t // tq),
        in_specs=[
            pl.BlockSpec((None, tq, pair_w), lambda b, gp, i: (b, i, gp)),
            pl.BlockSpec((None, tq, LANES), lambda b, gp, i: (b, i, gp)),
            pl.BlockSpec((None, kvc.shape[1], LANES), lambda b, gp, i: (b, 0, gp)),
            pl.BlockSpec((None, LANES, kvct.shape[2]), lambda b, gp, i: (b, npair + gp, 0)),
            kv_slab(0, 0), kv_slab(0, 1), kv_slab(1, 0), kv_slab(1, 1),
            full(e_t), full(aggt),
        ],
        out_specs=pl.BlockSpec((None, tq, pair_w), lambda b, gp, i: (b, i, gp)),
        out_shape=jax.ShapeDtypeStruct((n, t, npair * pair_w), BF16),
        compiler_params=_params("arbitrary", "arbitrary", "arbitrary"),
        name="attn_prompt",
    )(q2, gates2, kvc, kvct, kvtb, kvtb, kvtb, kvtb, e_t, aggt)


def _attn_sample_kernel(pt_ref, qbd_ref, g_ref, kvc_ref, nslc_ref, wst_ref, nwin_ref, et_ref, enew_ref, aggt_ref,
                        *rest, n_pages, past_len):
    page_refs = rest[:n_pages]
    z_ref, wout_ref = rest[n_pages:]
    seq = qbd_ref.shape[0]
    kw = KV_HALF
    rows = GROUP * N_KV_HEADS * seq
    w_len = wst_ref.shape[1]
    n_cmp = kvc_ref.shape[0]
    n_sel = -(-(n_pages * PAGE_SIZE + seq) // SEL_BLK)

    qbd = jnp.concatenate(
        [qbd_ref[:, (g * GROUP + r) * kw:(g * GROUP + r + 1) * kw] for r in range(GROUP) for g in range(N_KV_HEADS)],
        axis=0).astype(BF16)
    qpos_row = past_len + (lax.broadcasted_iota(jnp.int32, (1, rows), 1) & (seq - 1))
    qpos_col = past_len + (lax.broadcasted_iota(jnp.int32, (rows, 1), 0) & (seq - 1))

    cend = lax.broadcasted_iota(jnp.int32, (n_cmp, 1), 0) * CMP_STRIDE + (CMP_BLK - 1)
    pt = _softmax_cols(_dot_nt(kvc_ref[:, 0:kw], qbd), cend <= qpos_row)
    imp = _split_dot(aggt_ref[...], pt)
    per_r = N_KV_HEADS * seq
    imp_g = imp
    for r in range(1, GROUP):
        imp_g = imp_g + pltpu.roll(imp, r * per_r, axis=1)
    vct = kvc_ref[:, kw:2 * kw].astype(F32).T.astype(BF16)
    o_c = _dot(vct, pt.astype(BF16)).T

    selt = _top_n_rows(_block_scores(imp_g, jnp.broadcast_to(qpos_row, imp_g.shape), n_sel), min(TOP_N, n_sel))
    selm1 = (selt.T - 1.0).astype(BF16)
    qaug = jnp.concatenate([qbd, selm1], axis=1)

    pad = jnp.zeros((PAGE_SIZE - seq, kw), F32)
    new_tile = lambda ref, lo: jnp.concatenate([ref[:, lo:lo + kw], pad], axis=0).astype(BF16)

    s_tiles = []
    for p in range(n_pages):
        rhs = jnp.concatenate([page_refs[p][0:kw, :].astype(BF16), et_ref[:, p * PAGE_SIZE:(p + 1) * PAGE_SIZE]], axis=0)
        s_tiles.append(_dot(qaug, rhs))
    s_tiles.append(_dot_nt(qaug, jnp.concatenate([new_tile(nslc_ref, 0), enew_ref[...]], axis=1)))
    s_all = jnp.concatenate(s_tiles, axis=1)
    kpos = lax.broadcasted_iota(jnp.int32, (1, s_all.shape[1]), 1)
    es, inv_s = _softmax_rows_unnorm(s_all, kpos <= qpos_col)
    es = es.astype(BF16)
    o_s = _dot(es[:, n_pages * PAGE_SIZE:], new_tile(nslc_ref, kw))
    for p in range(n_pages):
        o_s = o_s + _dot_nt(es[:, p * PAGE_SIZE:(p + 1) * PAGE_SIZE], page_refs[p][kw:2 * kw, :].astype(BF16))
    o_s = o_s * inv_s

    wst = wst_ref[...]
    sw = jnp.concatenate([_dot(qbd, wst[0:kw, :].astype(BF16)), _dot_nt(qbd, new_tile(nwin_ref, 0))], axis=1)
    kpos_w = past_len - w_len + lax.broadcasted_iota(jnp.int32, (1, w_len + PAGE_SIZE), 1)
    mw = (kpos_w <= qpos_col) & (kpos_w >= qpos_col - WINDOW) & (kpos_w >= 0)
    ew, inv_w = _softmax_rows_unnorm(sw, mw)
    ew = ew.astype(BF16)
    o_w = (_dot_nt(ew[:, 0:w_len], wst[kw:2 * kw, :].astype(BF16)) + _dot(ew[:, w_len:], new_tile(nwin_ref, kw))) * inv_w

    gates = g_ref[...]
    half = lax.broadcasted_iota(jnp.int32, (seq, LANES), 1) // HEAD_DIM
    for r in range(GROUP):
        for g in range(N_KV_HEADS):
            head = g * GROUP + r
            rs = slice((r * N_KV_HEADS + g) * seq, (r * N_KV_HEADS + g + 1) * seq)
            cs = slice((g // 2) * LANES, (g // 2 + 1) * LANES)
            c = head * 3
            out = (gates[:, c:c + 1] * o_c[rs, cs] + gates[:, c + 1:c + 2] * o_s[rs, cs]
                   + gates[:, c + 2:c + 3] * o_w[rs, cs])
            z_ref[:, head * LANES:(head + 1) * LANES] = jnp.where(half == g % 2, out, 0.0)

    new_t = jnp.concatenate([nwin_ref[...], jnp.zeros((LANES - seq, KV_ROW), F32)], axis=0).T
    shifted = pltpu.roll(wst, w_len - seq, axis=1)
    lane = lax.broadcasted_iota(jnp.int32, (KV_ROW, LANES), 1)
    wout_ref[:, 0:w_len - LANES] = shifted[:, 0:w_len - LANES]
    wout_ref[:, w_len - LANES:w_len] = jnp.where(lane >= LANES - seq, pltpu.roll(new_t, LANES - seq, axis=1),
                                                 shifted[:, w_len - LANES:w_len])


def _attn_sample(page_table, qbd, gates, kvc, new_slc, cache_t, win_t, new_win, e_t, e_new, aggt, past_len):
    nb, seq, _ = qbd.shape
    n_pages = page_table.shape[0] // nb
    w_len = win_t.shape[2]
    assert w_len == min(WINDOW, past_len + seq) and w_len % LANES == 0
    full = lambda a: pl.BlockSpec(a.shape, lambda b, pt: (0,) * a.ndim)
    per_b = lambda a: pl.BlockSpec((None,) + a.shape[1:], lambda b, pt: (b,) + (0,) * (a.ndim - 1))
    page_specs = [pl.BlockSpec((None, KV_ROW, PAGE_SIZE),
                               lambda b, pt, p=p: (pt[b * n_pages + p], 0, 0)) for p in range(n_pages)]
    zw = N_HEADS * LANES
    return pl.pallas_call(
        functools.partial(_attn_sample_kernel, n_pages=n_pages, past_len=past_len),
        grid_spec=pltpu.PrefetchScalarGridSpec(
            num_scalar_prefetch=1,
            grid=(nb,),
            in_specs=[per_b(qbd), per_b(gates), per_b(kvc), per_b(new_slc), per_b(win_t), per_b(new_win),
                      full(e_t), full(e_new), full(aggt)] + page_specs,
            out_specs=[pl.BlockSpec((None, seq, zw), lambda b, pt: (b, 0, 0)),
                       pl.BlockSpec((None, KV_ROW, w_len), lambda b, pt: (b, 0, 0))],
        ),
        out_shape=[jax.ShapeDtypeStruct((nb, seq, zw), F32), jax.ShapeDtypeStruct((nb, KV_ROW, w_len), F32)],
        compiler_params=_params("arbitrary"),
        name="attn_sample",
    )(page_table, qbd, gates, kvc, new_slc, win_t, new_win, e_t, e_new, aggt, *([cache_t] * n_pages))


def _pad_cols(w, width):
    return jnp.pad(w, ((0, 0), (0, width - w.shape[1])))


def _pack_weights(b_in_w, b_out_w, cmp_w1, cmp_w2):
    d = b_in_w.shape[0]
    nq = N_HEADS * HEAD_DIM
    wq = b_in_w[:, :nq].reshape(d, N_KV_HEADS, GROUP, HEAD_DIM)
    wgate = b_in_w[:, nq:]
    eye2 = jnp.eye(2, dtype=F32)
    eye4 = jnp.eye(N_KV_HEADS, dtype=F32)
    par = jnp.arange(N_KV_HEADS) % 2
    sel2 = jax.nn.one_hot(par, 2, dtype=F32)
    wq_p = jnp.einsum('dgrh,ga->dgrah', wq, sel2).reshape(d, N_HEADS * LANES)
    wq_s = jnp.einsum('dgrh,ga->dgrah', wq, eye4).reshape(d, N_HEADS * N_KV_HEADS * HEAD_DIM)
    per_pair = 2 * GROUP * 3
    wg_p = jnp.concatenate([_pad_cols(wgate[:, i * per_pair:(i + 1) * per_pair], LANES)
                            for i in range(N_KV_HEADS // 2)], axis=1)
    wg_s = _pad_cols(wgate, LANES)
    wo = b_out_w.reshape(N_KV_HEADS, GROUP, HEAD_DIM, d)
    wo_pad = jnp.einsum('grhd,ga->grahd', wo, sel2).reshape(N_HEADS * LANES, d)
    w1 = cmp_w1.reshape(2, CMP_BLK // CMP_STRIDE, CMP_STRIDE, HEAD_DIM, -1)
    hid = w1.shape[-1]
    w1p = jnp.einsum('krjdh,ab->kjadrbh', w1, eye2).reshape(2, CMP_STRIDE * LANES, 2 * 2 * hid)
    w1d = jnp.concatenate([cmp_w1.reshape(2, CMP_BLK * HEAD_DIM, hid)] * 2, axis=2)
    w2p = jnp.einsum('khe,ab->kahbe', cmp_w2, eye2).reshape(2, 2 * hid, 2 * HEAD_DIM)
    cast = lambda a: a.astype(BF16)
    return tuple(map(cast, (wq_p, wq_s, wg_p, wg_s, wo_pad, w1p, w1d, w2p)))


def _agg_t(n_cmp, n_sel_pad):
    c0 = jnp.arange(n_cmp) * CMP_STRIDE
    s0 = jnp.arange(n_sel_pad) * SEL_BLK
    ov = jnp.clip(jnp.minimum(c0[None, :] + CMP_BLK, s0[:, None] + SEL_BLK) - jnp.maximum(c0[None, :], s0[:, None]),
                  0, None)
    return (ov.astype(F32) / CMP_STRIDE).astype(BF16)


def _block_onehot(key0, n_keys):
    blk = (key0 + jnp.arange(n_keys)) // SEL_BLK
    return (jax.nn.one_hot(blk, LANES, dtype=F32) * BIG).astype(BF16)


def _feature_major(a):
    return a.transpose(0, 2, 3, 4, 1).reshape(a.shape[0], KV_ROW, a.shape[1])


def _position_major(a_t):
    b, _, pos = a_t.shape
    return a_t.reshape(b, 2, N_KV_HEADS, HEAD_DIM, pos).transpose(0, 4, 1, 2, 3)


def kernel(x_prompt, x_sample, cache_cmp_kv, cache_slc_kv, state_win_kv, state_conv, page_table, norm_w, final_norm_w, a_in_w, a_conv_w, a_out_w, b_in_w, b_out_w, kv_norm_w, kv_w, cmp_pe, cmp_w1, cmp_w2, ffn_in_w, ffn_out_w):
    n, t, d = x_prompt.shape
    nb, ts, _ = x_sample.shape
    n_pages = page_table.shape[1]
    past_len = n_pages * PAGE_SIZE
    d_ff = ffn_out_w.shape[1]
    assert a_in_w.shape[0] == 1 and b_in_w.shape[0] == 1 and d == N_HEADS * HEAD_DIM
    assert t % (CMP_TILE * CMP_STRIDE) == 0 and t % KEY_CHUNK == 0 and KEY_CHUNK % Q_TILE == 0
    assert ts == SUBLANES and (nb * ts) % ROW_TILE == 0 and d_ff % (2 * LANES) == 0

    wq_p, wq_s, wg_p, wg_s, wo_pad, w1p, w1d, w2p = _pack_weights(b_in_w[0], b_out_w[0], cmp_w1, cmp_w2)
    a_in, a_out, kvw = a_in_w[0].astype(BF16), a_out_w[0].astype(BF16), kv_w.astype(BF16)
    ffn_in, ffn_out = ffn_in_w.astype(BF16), ffn_out_w.astype(BF16)
    nw = norm_w.reshape(norm_w.shape[0], 2, 1, d)
    fnw = final_norm_w.reshape(1, d)
    kvnw = kv_norm_w.reshape(1, d)
    pe = cmp_pe.reshape(2, 1, CMP_BLK * HEAD_DIM)
    mp, ms = n * t, nb * ts

    z0, tail_p = _l0_in_prompt(x_prompt, nw[0, 0], a_in, a_conv_w[0])
    h1 = _mix_ffn(x_prompt.reshape(mp, d), z0.reshape(mp, d), a_out, nw[0, 1], ffn_in[0], ffn_out[0], fnw, final=False)
    q2, g2, cmp_rows, cmp_t, slc_t, win_t, kvtb = _l1_in_prompt(h1.reshape(n, t, d), nw[1, 0], kvnw, wq_p, wg_p, kvw)
    n_chunks = t // CMP_STRIDE
    kvc, kvct = _compress_prompt(cmp_rows.reshape(n, n_chunks, CHUNK_LANES), pe, w1d, w1p, w2p)
    z1 = _attn_prompt(q2, g2, kvc, kvct, kvtb, _block_onehot(0, t).T, _agg_t(n_chunks, LANES))
    y_p = _mix_ffn(h1, z1.reshape(mp, -1), wo_pad, nw[1, 1], ffn_in[1], ffn_out[1], fnw, final=True)

    zero = jnp.zeros((nb, ts - 2, d), F32)
    pre1 = jnp.concatenate([state_conv[0, :, 1:2], jnp.zeros((nb, ts - 1, d), F32)], axis=1).reshape(ms, d)
    pre2 = jnp.concatenate([state_conv[0], zero], axis=1).reshape(ms, d)
    xs = x_sample.reshape(ms, d)
    z0s, v_s = _l0_in_sample(xs, pre1, pre2, nw[0, 0], a_in, a_conv_w[0], seq=ts)
    h1s = _mix_ffn(xs, z0s, a_out, nw[0, 1], ffn_in[0], ffn_out[0], fnw, final=False)
    qbd, gs, cmp_s, slc_s, win_s_new = _l1_in_sample(h1s, nw[1, 0], kvnw, wq_s, wg_s, kvw)
    pt_flat = page_table.reshape(-1)
    kvc_s = _compress_sample(pt_flat, _feature_major(cache_cmp_kv), cmp_s.reshape(nb, ts, KV_ROW), pe, w1d, w1p, w2p)
    z1s, win_s_t = _attn_sample(pt_flat, qbd.reshape(nb, ts, -1), gs.reshape(nb, ts, -1), kvc_s,
                                slc_s.reshape(nb, ts, KV_ROW), _feature_major(cache_slc_kv),
                                _feature_major(state_win_kv), win_s_new.reshape(nb, ts, KV_ROW),
                                _block_onehot(0, past_len).T, _block_onehot(past_len, PAGE_SIZE),
                                _agg_t(kvc_s.shape[1], LANES), past_len)
    y_s = _mix_ffn(h1s, z1s.reshape(ms, -1), wo_pad, nw[1, 1], ffn_in[1], ffn_out[1], fnw, final=True)

    kv_shape = (2, N_KV_HEADS, HEAD_DIM)
    keep_p = min(WINDOW, t)
    return (
        y_p.reshape(n, t, d),
        y_s.reshape(nb, ts, d),
        tail_p[:, SUBLANES - (CONV_W - 1):][None],
        v_s.reshape(nb, ts, d)[:, ts - (CONV_W - 1):][None],
        _position_major(cmp_t),
        cmp_s.reshape((nb, ts) + kv_shape),
        _position_major(slc_t),
        slc_s.reshape((nb, ts) + kv_shape),
        _position_major(win_t[:, :, t - keep_p:]),
        _position_major(win_s_t),
    )
```

```python
import functools

import jax
import jax.numpy as jnp
from jax import lax
from jax.experimental import pallas as pl
from jax.experimental.pallas import tpu as pltpu

F32 = jnp.float32
BF16 = jnp.bfloat16

HEAD_DIM = 64
N_HEADS = 16
N_KV_HEADS = 4
GROUP = N_HEADS // N_KV_HEADS
CMP_BLK = 32
CMP_STRIDE = 16
SEL_BLK = 64
TOP_N = 16
WINDOW = 512
PAGE_SIZE = 128
CONV_W = 3
RMS_EPS = 1e-6
BIG = 1e9
NEG = -1e30
LOG2E = 1.4426950408889634
M_INIT = -(2.0 ** 100)

LANES = 128
SUBLANES = 8
KV_ROW = 2 * N_KV_HEADS * HEAD_DIM
KV_HALF = KV_ROW // 2
CHUNK_LANES = CMP_STRIDE * KV_ROW
VMEM_LIMIT_BYTES = 56 * 1024 * 1024

ROW_TILE = 512
CMP_TILE = 128
Q_TILE = 128
KEY_CHUNK = 512
SAMPLE_BATCH_PER_STEP = 2

_NT = (((1,), (1,)), ((), ()))


def _dot(a, b):
    return jnp.dot(a, b, preferred_element_type=F32)


def _dot_nt(a, b):
    return lax.dot_general(a, b, _NT, preferred_element_type=F32)


def _rms_unit(x):
    return x * lax.rsqrt(jnp.mean(x * x, axis=-1, keepdims=True) + RMS_EPS)


def _silu(x):
    return x * jax.nn.sigmoid(x)


def _params(*sem):
    return pltpu.CompilerParams(dimension_semantics=sem, vmem_limit_bytes=VMEM_LIMIT_BYTES)


def _conv_gate(xn_bf16, win_ref, cw_ref, vm1_fix, vm2_fix, d):
    b = _dot(xn_bf16, win_ref[:, 0:d])
    c = _dot(xn_bf16, win_ref[:, d:2 * d])
    u = _dot(xn_bf16, win_ref[:, 2 * d:3 * d])
    v = c * u
    vm1 = vm1_fix(pltpu.roll(v, 1, axis=0))
    vm2 = vm2_fix(pltpu.roll(v, 2, axis=0))
    cw = cw_ref[...]
    conv = cw[0:1, :] * vm2 + cw[1:2, :] * vm1 + cw[2:3, :] * v
    return b * conv, v


def _l0_in_prompt_kernel(x_ref, nw_ref, win_ref, cw_ref, z_ref, tail_ref, carry_ref):
    i = pl.program_id(1)
    tm, d = x_ref.shape

    @pl.when(i == 0)
    def _():
        carry_ref[...] = jnp.zeros_like(carry_ref)

    xn = (_rms_unit(x_ref[...]) * nw_ref[...]).astype(BF16)
    prev = carry_ref[...]
    row = lax.broadcasted_iota(jnp.int32, (tm, 1), 0)
    fix1 = lambda r: jnp.where(row == 0, prev[7:8, :], r)
    fix2 = lambda r: jnp.where(row == 0, prev[6:7, :], jnp.where(row == 1, prev[7:8, :], r))
    z, v = _conv_gate(xn, win_ref, cw_ref, fix1, fix2, d)
    z_ref[...] = z.astype(z_ref.dtype)
    carry_ref[...] = v[tm - SUBLANES:tm, :]

    @pl.when(i == pl.num_programs(1) - 1)
    def _():
        tail_ref[...] = v[tm - SUBLANES:tm, :]


def _l0_in_sample_kernel(x_ref, p1_ref, p2_ref, nw_ref, win_ref, cw_ref, z_ref, v_ref, *, seq):
    tm, d = x_ref.shape
    xn = (_rms_unit(x_ref[...]) * nw_ref[...]).astype(BF16)
    t = lax.broadcasted_iota(jnp.int32, (tm, 1), 0) & (seq - 1)
    fix1 = lambda r: jnp.where(t == 0, p1_ref[...], r)
    fix2 = lambda r: jnp.where(t < 2, p2_ref[...], r)
    z, v = _conv_gate(xn, win_ref, cw_ref, fix1, fix2, d)
    z_ref[...] = z.astype(z_ref.dtype)
    v_ref[...] = v


def _l0_in_prompt(x, nw, w_in, conv_w):
    n, t, d = x.shape
    tm = ROW_TILE
    return pl.pallas_call(
        _l0_in_prompt_kernel,
        grid=(n, t // tm),
        in_specs=[
            pl.BlockSpec((None, tm, d), lambda b, i: (b, i, 0)),
            pl.BlockSpec((1, d), lambda b, i: (0, 0)),
            pl.BlockSpec((d, 3 * d), lambda b, i: (0, 0)),
            pl.BlockSpec((CONV_W, d), lambda b, i: (0, 0)),
        ],
        out_specs=[
            pl.BlockSpec((None, tm, d), lambda b, i: (b, i, 0)),
            pl.BlockSpec((None, SUBLANES, d), lambda b, i: (b, 0, 0)),
        ],
        out_shape=[jax.ShapeDtypeStruct((n, t, d), BF16), jax.ShapeDtypeStruct((n, SUBLANES, d), F32)],
        scratch_shapes=[pltpu.VMEM((SUBLANES, d), F32)],
        compiler_params=_params("arbitrary", "arbitrary"),
        name="l0_in_prompt",
    )(x, nw, w_in, conv_w)


def _l0_in_sample(x, pre1, pre2, nw, w_in, conv_w, seq):
    m, d = x.shape
    tm = ROW_TILE
    row = pl.BlockSpec((tm, d), lambda i: (i, 0))
    return pl.pallas_call(
        functools.partial(_l0_in_sample_kernel, seq=seq),
        grid=(m // tm,),
        in_specs=[row, row, row,
                  pl.BlockSpec((1, d), lambda i: (0, 0)),
                  pl.BlockSpec((d, 3 * d), lambda i: (0, 0)),
                  pl.BlockSpec((CONV_W, d), lambda i: (0, 0))],
        out_specs=[row, row],
        out_shape=[jax.ShapeDtypeStruct((m, d), BF16), jax.ShapeDtypeStruct((m, d), F32)],
        compiler_params=_params("arbitrary"),
        name="l0_in_sample",
    )(x, pre1, pre2, nw, w_in, conv_w)


def _mix_ffn_kernel(h_ref, z_ref, wo_ref, nw_ref, wg_ref, wu_ref, wd_ref, fnw_ref, o_ref, hn_ref, *, final):
    j = pl.program_id(1)

    @pl.when(j == 0)
    def _():
        h1 = h_ref[...] + _dot(z_ref[...].astype(BF16), wo_ref[...])
        o_ref[...] = h1
        hn_ref[...] = (_rms_unit(h1) * nw_ref[...]).astype(BF16)

    hn = hn_ref[...]
    act = _silu(_dot(hn, wg_ref[...])) * _dot(hn, wu_ref[...])
    o_ref[...] += _dot(act.astype(BF16), wd_ref[...])

    if final:
        @pl.when(j == pl.num_programs(1) - 1)
        def _():
            o_ref[...] = _rms_unit(o_ref[...]) * fnw_ref[...]


def _mix_ffn(h, z, wo, nw, w_in, w_out, fnw, final):
    m, d = h.shape
    kz = z.shape[1]
    f = w_out.shape[0]
    tm, tf = ROW_TILE, f // 2
    nf = f // tf
    return pl.pallas_call(
        functools.partial(_mix_ffn_kernel, final=final),
        grid=(m // tm, nf),
        in_specs=[
            pl.BlockSpec((tm, d), lambda i, j: (i, 0)),
            pl.BlockSpec((tm, kz), lambda i, j: (i, 0)),
            pl.BlockSpec((kz, d), lambda i, j: (0, 0)),
            pl.BlockSpec((1, d), lambda i, j: (0, 0)),
            pl.BlockSpec((d, tf), lambda i, j: (0, j)),
            pl.BlockSpec((d, tf), lambda i, j: (0, j + nf)),
            pl.BlockSpec((tf, d), lambda i, j: (j, 0)),
            pl.BlockSpec((1, d), lambda i, j: (0, 0)),
        ],
        out_specs=pl.BlockSpec((tm, d), lambda i, j: (i, 0)),
        out_shape=jax.ShapeDtypeStruct((m, d), F32),
        scratch_shapes=[pltpu.VMEM((tm, d), BF16)],
        compiler_params=_params("arbitrary", "arbitrary"),
        name="mix_ffn_final" if final else "mix_ffn",
    )(h, z, wo, nw, w_in, w_in, w_out, fnw)


def _l1_proj(h_ref, nw_ref, kvnw_ref, wq_ref, wg_ref, wkv_ref, q_ref, g_ref):
    unit = _rms_unit(h_ref[...])
    xn = (unit * nw_ref[...]).astype(BF16)
    kn = (unit * kvnw_ref[...]).astype(BF16)
    q_ref[...] = (_dot(xn, wq_ref[...]) * (HEAD_DIM ** -0.5 * LOG2E)).astype(q_ref.dtype)
    g_ref[...] = jax.nn.sigmoid(_dot(xn, wg_ref[...]))
    return _dot(kn, wkv_ref[...])


def _l1_in_prompt_kernel(h_ref, nw_ref, kvnw_ref, wq_ref, wg_ref, wkv_ref,
                         q_ref, g_ref, cmpr_ref, cmpt_ref, slct_ref, wint_ref, kvtb_ref):
    kv = _l1_proj(h_ref, nw_ref, kvnw_ref, wq_ref, wg_ref, wkv_ref, q_ref, g_ref)
    cmpr_ref[...] = kv[:, 0:KV_ROW]
    kvt = kv.T
    cmpt_ref[...] = kvt[0:KV_ROW, :]
    slct_ref[...] = kvt[KV_ROW:2 * KV_ROW, :]
    wint_ref[...] = kvt[2 * KV_ROW:3 * KV_ROW, :]
    kvtb_ref[...] = kvt[KV_ROW:3 * KV_ROW, :].astype(BF16)


def _l1_in_sample_kernel(h_ref, nw_ref, kvnw_ref, wq_ref, wg_ref, wkv_ref, q_ref, g_ref, cmp_ref, slc_ref, win_ref):
    kv = _l1_proj(h_ref, nw_ref, kvnw_ref, wq_ref, wg_ref, wkv_ref, q_ref, g_ref)
    cmp_ref[...] = kv[:, 0:KV_ROW]
    slc_ref[...] = kv[:, KV_ROW:2 * KV_ROW]
    win_ref[...] = kv[:, 2 * KV_ROW:3 * KV_ROW]


def _l1_in_prompt(h, nw, kvnw, wq, wg, wkv):
    n, t, d = h.shape
    tm = ROW_TILE
    nq, ng = wq.shape[1], wg.shape[1]
    row = lambda w: pl.BlockSpec((None, tm, w), lambda b, i: (b, i, 0))
    col = lambda r: pl.BlockSpec((None, r, tm), lambda b, i: (b, 0, i))
    full = lambda a: pl.BlockSpec(a.shape, lambda b, i: (0, 0))
    tshape = lambda r, dt: jax.ShapeDtypeStruct((n, r, t), dt)
    return pl.pallas_call(
        _l1_in_prompt_kernel,
        grid=(n, t // tm),
        in_specs=[row(d), full(nw), full(kvnw), full(wq), full(wg), full(wkv)],
        out_specs=[row(nq), row(ng), row(KV_ROW), col(KV_ROW), col(KV_ROW), col(KV_ROW), col(2 * KV_ROW)],
        out_shape=[jax.ShapeDtypeStruct((n, t, nq), BF16), jax.ShapeDtypeStruct((n, t, ng), F32),
                   jax.ShapeDtypeStruct((n, t, KV_ROW), F32),
                   tshape(KV_ROW, F32), tshape(KV_ROW, F32), tshape(KV_ROW, F32), tshape(2 * KV_ROW, BF16)],
        compiler_params=_params("arbitrary", "arbitrary"),
        name="l1_in_prompt",
    )(h, nw, kvnw, wq, wg, wkv)


def _l1_in_sample(h, nw, kvnw, wq, wg, wkv):
    m, d = h.shape
    tm = ROW_TILE
    nq, ng = wq.shape[1], wg.shape[1]
    row = lambda w: pl.BlockSpec((tm, w), lambda i: (i, 0))
    full = lambda a: pl.BlockSpec(a.shape, lambda i: (0, 0))
    return pl.pallas_call(
        _l1_in_sample_kernel,
        grid=(m // tm,),
        in_specs=[row(d), full(nw), full(kvnw), full(wq), full(wg), full(wkv)],
        out_specs=[row(nq), row(ng), row(KV_ROW), row(KV_ROW), row(KV_ROW)],
        out_shape=[jax.ShapeDtypeStruct((m, nq), F32), jax.ShapeDtypeStruct((m, ng), F32),
                   jax.ShapeDtypeStruct((m, KV_ROW), F32), jax.ShapeDtypeStruct((m, KV_ROW), F32),
                   jax.ShapeDtypeStruct((m, KV_ROW), F32)],
        compiler_params=_params("arbitrary"),
        name="l1_in_sample",
    )(h, nw, kvnw, wq, wg, wkv)


def _compress_tokens(gather, tc, pe_ref, w1d_ref, w1p_ref, w2_ref, shift_ref):
    pieces = []
    for p in range(KV_ROW // LANES):
        k = p // 2
        xp = jnp.concatenate([gather(p, j) for j in range(CMP_STRIDE)], axis=1).astype(BF16)
        part = _dot(xp, w1p_ref[k])
        shift_ref[...] = part[:, LANES:2 * LANES]
        peb = _dot(pe_ref[k].astype(BF16), w1d_ref[k])
        pre = part[0:tc, 0:LANES] + shift_ref[pl.ds(1, tc), :] + peb
        pieces.append(_dot(_silu(pre).astype(BF16), w2_ref[k]))
    return jnp.concatenate(pieces, axis=1)


def _compress_prompt_kernel(x_ref, halo_ref, pe_ref, w1d_ref, w1p_ref, w2_ref, kc_ref, kct_ref, xs_ref, shift_ref):
    tc = x_ref.shape[0]
    xs_ref[0:tc, :] = x_ref[...]
    xs_ref[tc:tc + SUBLANES, :] = halo_ref[...]
    gather = lambda p, j: xs_ref[:, (j * 4 + p) * LANES:(j * 4 + p + 1) * LANES]
    kc = _compress_tokens(gather, tc, pe_ref, w1d_ref, w1p_ref, w2_ref, shift_ref)
    kc_ref[...] = kc.astype(kc_ref.dtype)
    kct_ref[...] = kc.T.astype(kct_ref.dtype)


def _compress_prompt(chunks, pe, w1d, w1p, w2):
    n, c, _ = chunks.shape
    tc = CMP_TILE
    nhalo = c // SUBLANES
    full = lambda a: pl.BlockSpec(a.shape, lambda b, i: (0,) * a.ndim)
    return pl.pallas_call(
        _compress_prompt_kernel,
        grid=(n, c // tc),
        in_specs=[
            pl.BlockSpec((None, tc, CHUNK_LANES), lambda b, i: (b, i, 0)),
            pl.BlockSpec((None, SUBLANES, CHUNK_LANES),
                         lambda b, i: (b, jnp.minimum((i + 1) * (tc // SUBLANES), nhalo - 1), 0)),
            full(pe), full(w1d), full(w1p), full(w2),
        ],
        out_specs=[pl.BlockSpec((None, tc, KV_ROW), lambda b, i: (b, i, 0)),
                   pl.BlockSpec((None, KV_ROW, tc), lambda b, i: (b, 0, i))],
        out_shape=[jax.ShapeDtypeStruct((n, c, KV_ROW), BF16), jax.ShapeDtypeStruct((n, KV_ROW, c), BF16)],
        scratch_shapes=[pltpu.VMEM((tc + SUBLANES, CHUNK_LANES), F32), pltpu.VMEM((tc + SUBLANES, LANES), F32)],
        compiler_params=_params("arbitrary", "arbitrary"),
        name="compress_prompt",
    )(chunks, chunks, pe, w1d, w1p, w2)


def _compress_sample_kernel(pt_ref, new_ref, pe_ref, w1d_ref, w1p_ref, w2_ref, *rest, n_pages, bps):
    page_refs = rest[:bps * n_pages]
    kc_ref, xs_ref, shift_ref = rest[bps * n_pages:]
    for bi in range(bps):
        _compress_sample_one(new_ref.at[bi], pe_ref, w1d_ref, w1p_ref, w2_ref,
                             page_refs[bi * n_pages:(bi + 1) * n_pages], kc_ref.at[bi], xs_ref.at[bi], shift_ref.at[bi])


def _compress_sample_one(new_ref, pe_ref, w1d_ref, w1p_ref, w2_ref, page_refs, kc_ref, xs_ref, shift_ref):
    n_pages = len(page_refs)
    n_pieces = KV_ROW // LANES
    past = n_pages * PAGE_SIZE
    seq = new_ref.shape[0]
    tc = past // CMP_STRIDE
    rows = xs_ref.shape[1]
    for pg in range(n_pages):
        page = page_refs[pg][...]
        for p in range(n_pieces):
            xs_ref[p, pg * PAGE_SIZE:(pg + 1) * PAGE_SIZE, :] = page[p * LANES:(p + 1) * LANES, :].T
    for p in range(n_pieces):
        xs_ref[p, past:past + seq, :] = new_ref[:, p * LANES:(p + 1) * LANES]
        xs_ref[p, past + seq:rows, :] = jnp.zeros((rows - past - seq, LANES), F32)
    gather = lambda p, j: xs_ref[p, pl.ds(j, tc + SUBLANES, stride=CMP_STRIDE), :]
    kc = _compress_tokens(gather, tc, pe_ref, w1d_ref, w1p_ref, w2_ref, shift_ref)
    kc_ref[...] = kc.astype(kc_ref.dtype)


def _compress_sample(page_table, cache_t, new_rows, pe, w1d, w1p, w2):
    nb, seq, _ = new_rows.shape
    n_pages = page_table.shape[0] // nb
    bps = SAMPLE_BATCH_PER_STEP
    tc = n_pages * PAGE_SIZE // CMP_STRIDE
    full = lambda a: pl.BlockSpec(a.shape, lambda b, pt: (0,) * a.ndim)
    page_specs = [pl.BlockSpec((None, KV_ROW, PAGE_SIZE),
                               lambda b, pt, bi=bi, p=p: (pt[(b * bps + bi) * n_pages + p], 0, 0))
                  for bi in range(bps) for p in range(n_pages)]
    return pl.pallas_call(
        functools.partial(_compress_sample_kernel, n_pages=n_pages, bps=bps),
        grid_spec=pltpu.PrefetchScalarGridSpec(
            num_scalar_prefetch=1,
            grid=(nb // bps,),
            in_specs=[pl.BlockSpec((bps, seq, KV_ROW), lambda b, pt: (b, 0, 0)),
                      full(pe), full(w1d), full(w1p), full(w2)] + page_specs,
            out_specs=pl.BlockSpec((bps, tc, KV_ROW), lambda b, pt: (b, 0, 0)),
            scratch_shapes=[pltpu.VMEM((bps, KV_ROW // LANES, (tc + SUBLANES) * CMP_STRIDE, LANES), F32),
                            pltpu.VMEM((bps, tc + SUBLANES, LANES), F32)],
        ),
        out_shape=jax.ShapeDtypeStruct((nb, tc, KV_ROW), BF16),
        compiler_params=_params("arbitrary"),
        name="compress_sample",
    )(page_table, new_rows, pe, w1d, w1p, w2, *([cache_t] * (bps * n_pages)))


def _top_n_rows(score, n_pick):
    rows = lax.broadcasted_iota(jnp.int32, score.shape, 0).astype(F32)
    sel = jnp.zeros(score.shape, F32)
    s = score
    for _ in range(n_pick):
        m = jnp.max(s, axis=0, keepdims=True)
        first = jnp.min(jnp.where(s == m, rows, float(score.shape[0])), axis=0, keepdims=True)
        hit = rows == first
        sel = jnp.where(hit, 1.0, sel)
        s = jnp.where(hit, -jnp.inf, s)
    return sel


def _softmax_cols(s, mask):
    s = jnp.where(mask, s, NEG)
    m = jnp.max(s, axis=0, keepdims=True)
    e = jnp.where(mask, jnp.exp2(s - m), 0.0)
    return e * (1.0 / jnp.maximum(jnp.sum(e, axis=0, keepdims=True), 1e-30))


def _softmax_rows_unnorm(s, mask):
    s = jnp.where(mask, s, NEG)
    m = jnp.max(s, axis=1, keepdims=True)
    e = jnp.where(mask, jnp.exp2(s - m), 0.0)
    return e, 1.0 / jnp.maximum(jnp.sum(e, axis=1, keepdims=True), 1e-30)


def _edge_biases(tq):
    t = lax.broadcasted_iota(jnp.int32, (tq, tq), 0)
    i = lax.broadcasted_iota(jnp.int32, (tq, tq), 1)
    return jnp.where(i <= t, 0.0, NEG), jnp.where(i >= t, 0.0, NEG)


def _block_bias(d, causal, lower, lower_at):
    bias = jnp.where(d == 0, causal, jnp.where(d < 0, NEG, 0.0))
    if lower_at is not None:
        bias = jnp.where(d == lower_at, lower, bias)
    return bias


def _add_per_head(s, bias, n_heads):
    tq = bias.shape[0]
    return jnp.concatenate([s[h * tq:(h + 1) * tq] + bias for h in range(n_heads)], axis=0)


def _block_scores(imp, pos, n_sel):
    jj = lax.broadcasted_iota(jnp.int32, imp.shape, 0)
    valid = (jj * SEL_BLK <= pos) & (jj < n_sel)
    cur = pos // SEL_BLK
    forced = (jj == 0) | (jj == cur) | (jj == cur - 1)
    score = jnp.where(valid & forced, BIG, jnp.where(valid, imp, -BIG))
    return jnp.where(jj < n_sel, score, -jnp.inf)


def _split_dot(w_bf16, x):
    hi = x.astype(BF16)
    lo = (x - hi.astype(F32)).astype(BF16)
    return _dot(w_bf16, hi) + _dot(w_bf16, lo)


def _with_ones(v_half, ones, first):
    return jnp.concatenate([v_half, ones] if first else [ones, v_half], axis=0)


def _norm_by_other_half(acc):
    return acc * (1.0 / jnp.maximum(pltpu.roll(acc, HEAD_DIM, axis=1), 1e-30))


def _attn_prompt_kernel(q_ref, g_ref, kc_ref, vct_ref, skt_ref, svt_ref, wkt_ref, wvt_ref, et_ref, aggt_ref, z_ref):
    tq = q_ref.shape[0]
    tkc = KEY_CHUNK
    qt = pl.program_id(2)
    t0 = qt * tq
    heads = 2 * GROUP
    r_rows = heads * tq
    n_cmp = kc_ref.shape[0]
    n_sel = et_ref.shape[1] // SEL_BLK
    half = lax.broadcasted_iota(jnp.int32, (tq, LANES), 1) // HEAD_DIM
    qpos_row = t0 + (lax.broadcasted_iota(jnp.int32, (1, r_rows), 1) & (tq - 1))
    ones = jnp.ones((HEAD_DIM, tkc), BF16)

    qs = [q_ref[:, h * LANES:(h + 1) * LANES] for h in range(heads)]
    qall = jnp.concatenate(qs, axis=0)

    causal_b, lower_b = _edge_biases(tq)

    def pv_ones(p, v, ones_rows):
        return jnp.concatenate(
            [_dot_nt(p[gg * GROUP * tq:(gg + 1) * GROUP * tq],
                     _with_ones(v[gg * HEAD_DIM:(gg + 1) * HEAD_DIM], ones_rows, gg == 0)) for gg in range(2)], axis=0)

    cend = lax.broadcasted_iota(jnp.int32, (n_cmp, 1), 0) * CMP_STRIDE + (CMP_BLK - 1)
    pt = _softmax_cols(_dot_nt(kc_ref[...], qall), cend <= qpos_row)
    psum = []
    for gg in range(2):
        acc = pt[:, gg * GROUP * tq:(gg * GROUP + 1) * tq]
        for r in range(1, GROUP):
            acc = acc + pt[:, (gg * GROUP + r) * tq:(gg * GROUP + r + 1) * tq]
        psum.append(acc)
    imp = _split_dot(aggt_ref[...], jnp.concatenate(psum, axis=1))
    oct = _dot(vct_ref[...], pt.astype(BF16))

    n_wblk = WINDOW // tq + 1
    start = pl.multiple_of(jnp.maximum(t0 - WINDOW, 0), LANES)
    back0 = jnp.minimum(WINDOW // tq, qt)
    sw = jnp.concatenate(
        [_add_per_head(_dot(qall, wkt_ref[:, pl.ds(pl.multiple_of(start + b * tq, LANES), tq)]),
                       _block_bias(back0 - b, causal_b, lower_b, WINDOW // tq), heads) for b in range(n_wblk)],
        axis=1).astype(BF16)
    ew = jnp.exp2(sw - jnp.max(sw, axis=1, keepdims=True))
    o_w = _norm_by_other_half(pv_ones(ew, wvt_ref[:, pl.ds(start, n_wblk * tq)], jnp.ones((HEAD_DIM, n_wblk * tq), BF16)))

    gates = g_ref[...]
    part = [gates[:, 3 * h:3 * h + 1] * oct[:, h * tq:(h + 1) * tq].T
            + gates[:, 3 * h + 2:3 * h + 3] * o_w[h * tq:(h + 1) * tq, :] for h in range(heads)]

    pos = t0 + (lax.broadcasted_iota(jnp.int32, imp.shape, 1) & (tq - 1))
    selt = _top_n_rows(_block_scores(imp, pos, n_sel), min(TOP_N, n_sel))
    selm1 = [(selt[:, gg * tq:(gg + 1) * tq].T - 1.0).astype(BF16) for gg in range(2)]
    qaug = jnp.concatenate([jnp.concatenate([qs[h], selm1[h // GROUP]], axis=1) for h in range(heads)], axis=0)

    def sel_chunk(k0, carry, diag_block):
        m_i, acc = carry
        rhs = jnp.concatenate([skt_ref[:, pl.ds(k0, tkc)], et_ref[:, pl.ds(k0, tkc)]], axis=0)
        s = _dot(qaug, rhs)
        if diag_block is not None:
            s = jnp.concatenate(
                [_add_per_head(s[:, b * tq:(b + 1) * tq], _block_bias(diag_block - b, causal_b, lower_b, None), heads)
                 for b in range(tkc // tq)], axis=1)
        sb = s.astype(BF16)
        m_new = jnp.maximum(m_i, jnp.max(sb, axis=1, keepdims=True).astype(F32))
        alpha = jnp.exp2(m_i - m_new)
        p = jnp.exp2(sb - m_new.astype(BF16))
        return m_new, alpha * acc + pv_ones(p, svt_ref[:, pl.ds(k0, tkc)], ones)

    n_full = t0 // tkc
    init = (jnp.full((r_rows, 1), M_INIT, F32), jnp.zeros((r_rows, LANES), F32))
    carry = lax.fori_loop(0, n_full, lambda c, cr: sel_chunk(pl.multiple_of(c * tkc, tkc), cr, None), init)
    _, acc_s = sel_chunk(pl.multiple_of(n_full * tkc, tkc), carry, qt - n_full * (tkc // tq))
    o_s = _norm_by_other_half(acc_s)

    head_out = [part[h] + gates[:, 3 * h + 1:3 * h + 2] * o_s[h * tq:(h + 1) * tq, :] for h in range(heads)]
    for i in range(heads // 2):
        gg = (2 * i) // GROUP
        a, b = head_out[2 * i], head_out[2 * i + 1]
        packed = (jnp.where(half == 0, a, pltpu.roll(b, HEAD_DIM, axis=1)) if gg == 0
                  else jnp.where(half == 0, pltpu.roll(a, HEAD_DIM, axis=1), b))
        z_ref[:, i * LANES:(i + 1) * LANES] = packed.astype(z_ref.dtype)


def _attn_prompt(q2, gates2, kvc, kvct, kvtb, e_t, aggt):
    n, t, _ = q2.shape
    tq = Q_TILE
    npair = N_KV_HEADS // 2
    pair_w = 2 * GROUP * LANES
    kv_slab = lambda branch, kv: pl.BlockSpec(
        (None, LANES, t), lambda b, gp, i: (b, branch * 2 * npair + kv * npair + gp, 0))
    full = lambda a: pl.BlockSpec(a.shape, lambda b, gp, i: (0,) * a.ndim)
    return pl.pallas_call(
        _attn_prompt_kernel,
        grid=(n, npair, t // tq),
        in_specs=[
            pl.BlockSpec((None, tq, pair_w), lambda b, gp, i: (b, i, gp)),
            pl.BlockSpec((None, tq, LANES), lambda b, gp, i: (b, i, gp)),
            pl.BlockSpec((None, kvc.shape[1], LANES), lambda b, gp, i: (b, 0, gp)),
            pl.BlockSpec((None, LANES, kvct.shape[2]), lambda b, gp, i: (b, npair + gp, 0)),
            kv_slab(0, 0), kv_slab(0, 1), kv_slab(1, 0), kv_slab(1, 1),
            full(e_t), full(aggt),
        ],
        out_specs=pl.BlockSpec((None, tq, 2 * GROUP * HEAD_DIM), lambda b, gp, i: (b, i, gp)),
        out_shape=jax.ShapeDtypeStruct((n, t, N_HEADS * HEAD_DIM), BF16),
        compiler_params=_params("arbitrary", "arbitrary", "arbitrary"),
        name="attn_prompt",
    )(q2, gates2, kvc, kvct, kvtb, kvtb, kvtb, kvtb, e_t, aggt)


def _attn_sample_kernel(pt_ref, qbd_ref, g_ref, kvc_ref, nslc_ref, wst_ref, nwin_ref, et_ref, enew_ref, aggt_ref,
                        *rest, n_pages, past_len, bps):
    page_refs = rest[:bps * n_pages]
    z_ref, wout_ref = rest[bps * n_pages:]
    for bi in range(bps):
        _attn_sample_one(qbd_ref.at[bi], g_ref.at[bi], kvc_ref.at[bi], nslc_ref.at[bi], wst_ref.at[bi],
                         nwin_ref.at[bi], et_ref, enew_ref, aggt_ref, page_refs[bi * n_pages:(bi + 1) * n_pages],
                         z_ref.at[bi], wout_ref.at[bi], past_len)


def _attn_sample_one(qbd_ref, g_ref, kvc_ref, nslc_ref, wst_ref, nwin_ref, et_ref, enew_ref, aggt_ref, page_refs,
                     z_ref, wout_ref, past_len):
    n_pages = len(page_refs)
    seq = qbd_ref.shape[0]
    kw = KV_HALF
    rows = GROUP * N_KV_HEADS * seq
    w_len = wst_ref.shape[1]
    n_cmp = kvc_ref.shape[0]
    n_sel = -(-(n_pages * PAGE_SIZE + seq) // SEL_BLK)

    qbd = jnp.concatenate(
        [qbd_ref[:, (g * GROUP + r) * kw:(g * GROUP + r + 1) * kw] for r in range(GROUP) for g in range(N_KV_HEADS)],
        axis=0).astype(BF16)
    qpos_row = past_len + (lax.broadcasted_iota(jnp.int32, (1, rows), 1) & (seq - 1))
    qpos_col = past_len + (lax.broadcasted_iota(jnp.int32, (rows, 1), 0) & (seq - 1))

    cend = lax.broadcasted_iota(jnp.int32, (n_cmp, 1), 0) * CMP_STRIDE + (CMP_BLK - 1)
    pt = _softmax_cols(_dot_nt(kvc_ref[:, 0:kw], qbd), cend <= qpos_row)
    imp = _split_dot(aggt_ref[...], pt)
    per_r = N_KV_HEADS * seq
    imp_g = imp
    for r in range(1, GROUP):
        imp_g = imp_g + pltpu.roll(imp, r * per_r, axis=1)
    vct = kvc_ref[:, kw:2 * kw].astype(F32).T.astype(BF16)
    o_c = _dot(vct, pt.astype(BF16)).T

    selt = _top_n_rows(_block_scores(imp_g, jnp.broadcast_to(qpos_row, imp_g.shape), n_sel), min(TOP_N, n_sel))
    selm1 = (selt.T - 1.0).astype(BF16)
    qaug = jnp.concatenate([qbd, selm1], axis=1)

    pad = jnp.zeros((PAGE_SIZE - seq, kw), F32)
    new_tile = lambda ref, lo: jnp.concatenate([ref[:, lo:lo + kw], pad], axis=0).astype(BF16)

    s_tiles = []
    for p in range(n_pages):
        rhs = jnp.concatenate([page_refs[p][0:kw, :].astype(BF16), et_ref[:, p * PAGE_SIZE:(p + 1) * PAGE_SIZE]], axis=0)
        s_tiles.append(_dot(qaug, rhs))
    s_tiles.append(_dot_nt(qaug, jnp.concatenate([new_tile(nslc_ref, 0), enew_ref[...]], axis=1)))
    s_all = jnp.concatenate(s_tiles, axis=1)
    kpos = lax.broadcasted_iota(jnp.int32, (1, s_all.shape[1]), 1)
    es, inv_s = _softmax_rows_unnorm(s_all, kpos <= qpos_col)
    es = es.astype(BF16)
    o_s = _dot(es[:, n_pages * PAGE_SIZE:], new_tile(nslc_ref, kw))
    for p in range(n_pages):
        o_s = o_s + _dot_nt(es[:, p * PAGE_SIZE:(p + 1) * PAGE_SIZE], page_refs[p][kw:2 * kw, :].astype(BF16))
    o_s = o_s * inv_s

    wst = wst_ref[...]
    sw = jnp.concatenate([_dot(qbd, wst[0:kw, :].astype(BF16)), _dot_nt(qbd, new_tile(nwin_ref, 0))], axis=1)
    kpos_w = past_len - w_len + lax.broadcasted_iota(jnp.int32, (1, w_len + PAGE_SIZE), 1)
    mw = (kpos_w <= qpos_col) & (kpos_w >= qpos_col - WINDOW) & (kpos_w >= 0)
    ew, inv_w = _softmax_rows_unnorm(sw, mw)
    ew = ew.astype(BF16)
    o_w = (_dot_nt(ew[:, 0:w_len], wst[kw:2 * kw, :].astype(BF16)) + _dot(ew[:, w_len:], new_tile(nwin_ref, kw))) * inv_w

    gates = g_ref[...]
    half = lax.broadcasted_iota(jnp.int32, (seq, LANES), 1) // HEAD_DIM

    def head_out(g, r):
        rs = slice((r * N_KV_HEADS + g) * seq, (r * N_KV_HEADS + g + 1) * seq)
        cs = slice((g // 2) * LANES, (g // 2 + 1) * LANES)
        c = (g * GROUP + r) * 3
        return (gates[:, c:c + 1] * o_c[rs, cs] + gates[:, c + 1:c + 2] * o_s[rs, cs]
                + gates[:, c + 2:c + 3] * o_w[rs, cs])

    for g in range(N_KV_HEADS):
        for i in range(GROUP // 2):
            a, b = head_out(g, 2 * i), head_out(g, 2 * i + 1)
            packed = (jnp.where(half == 0, a, pltpu.roll(b, HEAD_DIM, axis=1)) if g % 2 == 0
                      else jnp.where(half == 0, pltpu.roll(a, HEAD_DIM, axis=1), b))
            col = g * (GROUP // 2) + i
            z_ref[:, col * LANES:(col + 1) * LANES] = packed

    new_t = jnp.concatenate([nwin_ref[...], jnp.zeros((LANES - seq, KV_ROW), F32)], axis=0).T
    shifted = pltpu.roll(wst, w_len - seq, axis=1)
    lane = lax.broadcasted_iota(jnp.int32, (KV_ROW, LANES), 1)
    wout_ref[:, 0:w_len - LANES] = shifted[:, 0:w_len - LANES]
    wout_ref[:, w_len - LANES:w_len] = jnp.where(lane >= LANES - seq, pltpu.roll(new_t, LANES - seq, axis=1),
                                                 shifted[:, w_len - LANES:w_len])


def _attn_sample(page_table, qbd, gates, kvc, new_slc, cache_t, win_t, new_win, e_t, e_new, aggt, past_len):
    nb, seq, _ = qbd.shape
    n_pages = page_table.shape[0] // nb
    w_len = win_t.shape[2]
    assert w_len == min(WINDOW, past_len + seq) and w_len % LANES == 0
    bps = SAMPLE_BATCH_PER_STEP
    full = lambda a: pl.BlockSpec(a.shape, lambda b, pt: (0,) * a.ndim)
    per_b = lambda a: pl.BlockSpec((bps,) + a.shape[1:], lambda b, pt: (b,) + (0,) * (a.ndim - 1))
    page_specs = [pl.BlockSpec((None, KV_ROW, PAGE_SIZE),
                               lambda b, pt, bi=bi, p=p: (pt[(b * bps + bi) * n_pages + p], 0, 0))
                  for bi in range(bps) for p in range(n_pages)]
    zw = N_HEADS * HEAD_DIM
    return pl.pallas_call(
        functools.partial(_attn_sample_kernel, n_pages=n_pages, past_len=past_len, bps=bps),
        grid_spec=pltpu.PrefetchScalarGridSpec(
            num_scalar_prefetch=1,
            grid=(nb // bps,),
            in_specs=[per_b(qbd), per_b(gates), per_b(kvc), per_b(new_slc), per_b(win_t), per_b(new_win),
                      full(e_t), full(e_new), full(aggt)] + page_specs,
            out_specs=[pl.BlockSpec((bps, seq, zw), lambda b, pt: (b, 0, 0)),
                       pl.BlockSpec((bps, KV_ROW, w_len), lambda b, pt: (b, 0, 0))],
        ),
        out_shape=[jax.ShapeDtypeStruct((nb, seq, zw), F32), jax.ShapeDtypeStruct((nb, KV_ROW, w_len), F32)],
        compiler_params=_params("arbitrary"),
        name="attn_sample",
    )(page_table, qbd, gates, kvc, new_slc, win_t, new_win, e_t, e_new, aggt, *([cache_t] * (bps * n_pages)))


def _pad_cols(w, width):
    return jnp.pad(w, ((0, 0), (0, width - w.shape[1])))


def _pack_weights(b_in_w, cmp_w1, cmp_w2):
    d = b_in_w.shape[0]
    nq = N_HEADS * HEAD_DIM
    wq = b_in_w[:, :nq].reshape(d, N_KV_HEADS, GROUP, HEAD_DIM)
    wgate = b_in_w[:, nq:]
    eye2 = jnp.eye(2, dtype=F32)
    eye4 = jnp.eye(N_KV_HEADS, dtype=F32)
    par = jnp.arange(N_KV_HEADS) % 2
    sel2 = jax.nn.one_hot(par, 2, dtype=F32)
    wq_p = jnp.einsum('dgrh,ga->dgrah', wq, sel2).reshape(d, N_HEADS * LANES)
    wq_s = jnp.einsum('dgrh,ga->dgrah', wq, eye4).reshape(d, N_HEADS * N_KV_HEADS * HEAD_DIM)
    per_pair = 2 * GROUP * 3
    wg_p = jnp.concatenate([_pad_cols(wgate[:, i * per_pair:(i + 1) * per_pair], LANES)
                            for i in range(N_KV_HEADS // 2)], axis=1)
    wg_s = _pad_cols(wgate, LANES)
    w1 = cmp_w1.reshape(2, CMP_BLK // CMP_STRIDE, CMP_STRIDE, HEAD_DIM, -1)
    hid = w1.shape[-1]
    w1p = jnp.einsum('krjdh,ab->kjadrbh', w1, eye2).reshape(2, CMP_STRIDE * LANES, 2 * 2 * hid)
    w1d = jnp.concatenate([cmp_w1.reshape(2, CMP_BLK * HEAD_DIM, hid)] * 2, axis=2)
    w2p = jnp.einsum('khe,ab->kahbe', cmp_w2, eye2).reshape(2, 2 * hid, 2 * HEAD_DIM)
    cast = lambda a: a.astype(BF16)
    return tuple(map(cast, (wq_p, wq_s, wg_p, wg_s, w1p, w1d, w2p)))


def _agg_t(n_cmp, n_sel_pad):
    c0 = jnp.arange(n_cmp) * CMP_STRIDE
    s0 = jnp.arange(n_sel_pad) * SEL_BLK
    ov = jnp.clip(jnp.minimum(c0[None, :] + CMP_BLK, s0[:, None] + SEL_BLK) - jnp.maximum(c0[None, :], s0[:, None]),
                  0, None)
    return (ov.astype(F32) / CMP_STRIDE).astype(BF16)


def _block_onehot(key0, n_keys):
    blk = (key0 + jnp.arange(n_keys)) // SEL_BLK
    return (jax.nn.one_hot(blk, LANES, dtype=F32) * BIG).astype(BF16)


def _feature_major(a):
    return a.transpose(0, 2, 3, 4, 1).reshape(a.shape[0], KV_ROW, a.shape[1])


def _position_major(a_t):
    b, _, pos = a_t.shape
    return a_t.reshape(b, 2, N_KV_HEADS, HEAD_DIM, pos).transpose(0, 4, 1, 2, 3)


def kernel(x_prompt, x_sample, cache_cmp_kv, cache_slc_kv, state_win_kv, state_conv, page_table, norm_w, final_norm_w, a_in_w, a_conv_w, a_out_w, b_in_w, b_out_w, kv_norm_w, kv_w, cmp_pe, cmp_w1, cmp_w2, ffn_in_w, ffn_out_w):
    n, t, d = x_prompt.shape
    nb, ts, _ = x_sample.shape
    n_pages = page_table.shape[1]
    past_len = n_pages * PAGE_SIZE
    d_ff = ffn_out_w.shape[1]
    assert a_in_w.shape[0] == 1 and b_in_w.shape[0] == 1 and d == N_HEADS * HEAD_DIM
    assert t % (CMP_TILE * CMP_STRIDE) == 0 and t % KEY_CHUNK == 0 and KEY_CHUNK % Q_TILE == 0
    assert ts == SUBLANES and (nb * ts) % ROW_TILE == 0 and d_ff % (2 * LANES) == 0
    assert nb % SAMPLE_BATCH_PER_STEP == 0 and WINDOW % Q_TILE == 0

    wq_p, wq_s, wg_p, wg_s, w1p, w1d, w2p = _pack_weights(b_in_w[0], cmp_w1, cmp_w2)
    a_in, a_out, kvw = a_in_w[0].astype(BF16), a_out_w[0].astype(BF16), kv_w.astype(BF16)
    b_out = b_out_w[0].astype(BF16)
    ffn_in, ffn_out = ffn_in_w.astype(BF16), ffn_out_w.astype(BF16)
    nw = norm_w.reshape(norm_w.shape[0], 2, 1, d)
    fnw = final_norm_w.reshape(1, d)
    kvnw = kv_norm_w.reshape(1, d)
    pe = cmp_pe.reshape(2, 1, CMP_BLK * HEAD_DIM)
    mp, ms = n * t, nb * ts

    z0, tail_p = _l0_in_prompt(x_prompt, nw[0, 0], a_in, a_conv_w[0])
    h1 = _mix_ffn(x_prompt.reshape(mp, d), z0.reshape(mp, d), a_out, nw[0, 1], ffn_in[0], ffn_out[0], fnw, final=False)
    q2, g2, cmp_rows, cmp_t, slc_t, win_t, kvtb = _l1_in_prompt(h1.reshape(n, t, d), nw[1, 0], kvnw, wq_p, wg_p, kvw)
    n_chunks = t // CMP_STRIDE
    kvc, kvct = _compress_prompt(cmp_rows.reshape(n, n_chunks, CHUNK_LANES), pe, w1d, w1p, w2p)
    z1 = _attn_prompt(q2, g2, kvc, kvct, kvtb, _block_onehot(0, t).T, _agg_t(n_chunks, LANES))
    y_p = _mix_ffn(h1, z1.reshape(mp, -1), b_out, nw[1, 1], ffn_in[1], ffn_out[1], fnw, final=True)

    zero = jnp.zeros((nb, ts - 2, d), F32)
    pre1 = jnp.concatenate([state_conv[0, :, 1:2], jnp.zeros((nb, ts - 1, d), F32)], axis=1).reshape(ms, d)
    pre2 = jnp.concatenate([state_conv[0], zero], axis=1).reshape(ms, d)
    xs = x_sample.reshape(ms, d)
    z0s, v_s = _l0_in_sample(xs, pre1, pre2, nw[0, 0], a_in, a_conv_w[0], seq=ts)
    h1s = _mix_ffn(xs, z0s, a_out, nw[0, 1], ffn_in[0], ffn_out[0], fnw, final=False)
    qbd, gs, cmp_s, slc_s, win_s_new = _l1_in_sample(h1s, nw[1, 0], kvnw, wq_s, wg_s, kvw)
    pt_flat = page_table.reshape(-1)
    kvc_s = _compress_sample(pt_flat, _feature_major(cache_cmp_kv), cmp_s.reshape(nb, ts, KV_ROW), pe, w1d, w1p, w2p)
    z1s, win_s_t = _attn_sample(pt_flat, qbd.reshape(nb, ts, -1), gs.reshape(nb, ts, -1), kvc_s,
                                slc_s.reshape(nb, ts, KV_ROW), _feature_major(cache_slc_kv),
                                _feature_major(state_win_kv), win_s_new.reshape(nb, ts, KV_ROW),
                                _block_onehot(0, past_len).T, _block_onehot(past_len, PAGE_SIZE),
                                _agg_t(kvc_s.shape[1], LANES), past_len)
    y_s = _mix_ffn(h1s, z1s.reshape(ms, -1), b_out, nw[1, 1], ffn_in[1], ffn_out[1], fnw, final=True)

    kv_shape = (2, N_KV_HEADS, HEAD_DIM)
    keep_p = min(WINDOW, t)
    return (
        y_p.reshape(n, t, d),
        y_s.reshape(nb, ts, d),
        tail_p[:, SUBLANES - (CONV_W - 1):][None],
        v_s.reshape(nb, ts, d)[:, ts - (CONV_W - 1):][None],
        _position_major(cmp_t),
        cmp_s.reshape((nb, ts) + kv_shape),
        _position_major(slc_t),
        slc_s.reshape((nb, ts) + kv_shape),
        _position_major(win_t[:, :, t - keep_p:]),
        _position_major(win_s_t),
    )
```

```python
import functools

import jax
import jax.numpy as jnp
from jax import lax
from jax.experimental import pallas as pl
from jax.experimental.pallas import tpu as pltpu

F32 = jnp.float32
BF16 = jnp.bfloat16

HEAD_DIM = 64
N_HEADS = 16
N_KV_HEADS = 4
GROUP = N_HEADS // N_KV_HEADS
CMP_BLK = 32
CMP_STRIDE = 16
SEL_BLK = 64
TOP_N = 16
WINDOW = 512
PAGE_SIZE = 128
CONV_W = 3
RMS_EPS = 1e-6
BIG = 1e9
NEG = -1e30
LOG2E = 1.4426950408889634
M_INIT = -(2.0 ** 100)

LANES = 128
SUBLANES = 8
KV_ROW = 2 * N_KV_HEADS * HEAD_DIM
KV_HALF = KV_ROW // 2
CHUNK_LANES = CMP_STRIDE * KV_ROW
VMEM_LIMIT_BYTES = 56 * 1024 * 1024

ROW_TILE = 512
CMP_TILE = 128
Q_TILE = 128
KEY_CHUNK = 512
SAMPLE_BATCH_PER_STEP = 2

_NT = (((1,), (1,)), ((), ()))


def _dot(a, b):
    return jnp.dot(a, b, preferred_element_type=F32)


def _dot_nt(a, b):
    return lax.dot_general(a, b, _NT, preferred_element_type=F32)


def _rms_unit(x):
    return x * lax.rsqrt(jnp.mean(x * x, axis=-1, keepdims=True) + RMS_EPS)


def _silu(x):
    return x * jax.nn.sigmoid(x)


def _params(*sem):
    return pltpu.CompilerParams(dimension_semantics=sem, vmem_limit_bytes=VMEM_LIMIT_BYTES)


def _conv_gate(xn_bf16, win_ref, cw_ref, vm1_fix, vm2_fix, d):
    b = _dot(xn_bf16, win_ref[:, 0:d])
    c = _dot(xn_bf16, win_ref[:, d:2 * d])
    u = _dot(xn_bf16, win_ref[:, 2 * d:3 * d])
    v = c * u
    vm1 = vm1_fix(pltpu.roll(v, 1, axis=0))
    vm2 = vm2_fix(pltpu.roll(v, 2, axis=0))
    cw = cw_ref[...]
    conv = cw[0:1, :] * vm2 + cw[1:2, :] * vm1 + cw[2:3, :] * v
    return b * conv, v


def _l0_in_prompt_kernel(x_ref, nw_ref, win_ref, cw_ref, z_ref, tail_ref, carry_ref):
    i = pl.program_id(1)
    tm, d = x_ref.shape

    @pl.when(i == 0)
    def _():
        carry_ref[...] = jnp.zeros_like(carry_ref)

    xn = (_rms_unit(x_ref[...]) * nw_ref[...]).astype(BF16)
    prev = carry_ref[...]
    row = lax.broadcasted_iota(jnp.int32, (tm, 1), 0)
    fix1 = lambda r: jnp.where(row == 0, prev[7:8, :], r)
    fix2 = lambda r: jnp.where(row == 0, prev[6:7, :], jnp.where(row == 1, prev[7:8, :], r))
    z, v = _conv_gate(xn, win_ref, cw_ref, fix1, fix2, d)
    z_ref[...] = z.astype(z_ref.dtype)
    carry_ref[...] = v[tm - SUBLANES:tm, :]

    @pl.when(i == pl.num_programs(1) - 1)
    def _():
        tail_ref[...] = v[tm - SUBLANES:tm, :]


def _l0_in_sample_kernel(x_ref, p1_ref, p2_ref, nw_ref, win_ref, cw_ref, z_ref, v_ref, *, seq):
    tm, d = x_ref.shape
    xn = (_rms_unit(x_ref[...]) * nw_ref[...]).astype(BF16)
    t = lax.broadcasted_iota(jnp.int32, (tm, 1), 0) & (seq - 1)
    fix1 = lambda r: jnp.where(t == 0, p1_ref[...], r)
    fix2 = lambda r: jnp.where(t < 2, p2_ref[...], r)
    z, v = _conv_gate(xn, win_ref, cw_ref, fix1, fix2, d)
    z_ref[...] = z.astype(z_ref.dtype)
    v_ref[...] = v


def _l0_in_prompt(x, nw, w_in, conv_w):
    n, t, d = x.shape
    tm = ROW_TILE
    return pl.pallas_call(
        _l0_in_prompt_kernel,
        grid=(n, t // tm),
        in_specs=[
            pl.BlockSpec((None, tm, d), lambda b, i: (b, i, 0)),
            pl.BlockSpec((1, d), lambda b, i: (0, 0)),
            pl.BlockSpec((d, 3 * d), lambda b, i: (0, 0)),
            pl.BlockSpec((CONV_W, d), lambda b, i: (0, 0)),
        ],
        out_specs=[
            pl.BlockSpec((None, tm, d), lambda b, i: (b, i, 0)),
            pl.BlockSpec((None, SUBLANES, d), lambda b, i: (b, 0, 0)),
        ],
        out_shape=[jax.ShapeDtypeStruct((n, t, d), BF16), jax.ShapeDtypeStruct((n, SUBLANES, d), F32)],
        scratch_shapes=[pltpu.VMEM((SUBLANES, d), F32)],
        compiler_params=_params("arbitrary", "arbitrary"),
        name="l0_in_prompt",
    )(x, nw, w_in, conv_w)


def _l0_in_sample(x, pre1, pre2, nw, w_in, conv_w, seq):
    m, d = x.shape
    tm = ROW_TILE
    row = pl.BlockSpec((tm, d), lambda i: (i, 0))
    return pl.pallas_call(
        functools.partial(_l0_in_sample_kernel, seq=seq),
        grid=(m // tm,),
        in_specs=[row, row, row,
                  pl.BlockSpec((1, d), lambda i: (0, 0)),
                  pl.BlockSpec((d, 3 * d), lambda i: (0, 0)),
                  pl.BlockSpec((CONV_W, d), lambda i: (0, 0))],
        out_specs=[row, row],
        out_shape=[jax.ShapeDtypeStruct((m, d), BF16), jax.ShapeDtypeStruct((m, d), F32)],
        compiler_params=_params("arbitrary"),
        name="l0_in_sample",
    )(x, pre1, pre2, nw, w_in, conv_w)


def _mix_ffn_kernel(h_ref, z_ref, wo_ref, nw_ref, wg_ref, wu_ref, wd_ref, fnw_ref, o_ref, hn_ref, *, final):
    j = pl.program_id(1)

    @pl.when(j == 0)
    def _():
        h1 = h_ref[...] + _dot(z_ref[...].astype(BF16), wo_ref[...])
        o_ref[...] = h1
        hn_ref[...] = (_rms_unit(h1) * nw_ref[...]).astype(BF16)

    hn = hn_ref[...]
    act = _silu(_dot(hn, wg_ref[...])) * _dot(hn, wu_ref[...])
    o_ref[...] += _dot(act.astype(BF16), wd_ref[...])

    if final:
        @pl.when(j == pl.num_programs(1) - 1)
        def _():
            o_ref[...] = _rms_unit(o_ref[...]) * fnw_ref[...]


def _mix_ffn(h, z, wo, nw, w_in, w_out, fnw, final):
    m, d = h.shape
    kz = z.shape[1]
    f = w_out.shape[0]
    tm, tf = ROW_TILE, f // 2
    nf = f // tf
    return pl.pallas_call(
        functools.partial(_mix_ffn_kernel, final=final),
        grid=(m // tm, nf),
        in_specs=[
            pl.BlockSpec((tm, d), lambda i, j: (i, 0)),
            pl.BlockSpec((tm, kz), lambda i, j: (i, 0)),
            pl.BlockSpec((kz, d), lambda i, j: (0, 0)),
            pl.BlockSpec((1, d), lambda i, j: (0, 0)),
            pl.BlockSpec((d, tf), lambda i, j: (0, j)),
            pl.BlockSpec((d, tf), lambda i, j: (0, j + nf)),
            pl.BlockSpec((tf, d), lambda i, j: (j, 0)),
            pl.BlockSpec((1, d), lambda i, j: (0, 0)),
        ],
        out_specs=pl.BlockSpec((tm, d), lambda i, j: (i, 0)),
        out_shape=jax.ShapeDtypeStruct((m, d), F32),
        scratch_shapes=[pltpu.VMEM((tm, d), BF16)],
        compiler_params=_params("arbitrary", "arbitrary"),
        name="mix_ffn_final" if final else "mix_ffn",
    )(h, z, wo, nw, w_in, w_in, w_out, fnw)


def _l1_proj(h_ref, nw_ref, kvnw_ref, wq_ref, wg_ref, wkv_ref, q_ref, g_ref):
    unit = _rms_unit(h_ref[...])
    xn = (unit * nw_ref[...]).astype(BF16)
    kn = (unit * kvnw_ref[...]).astype(BF16)
    q_ref[...] = (_dot(xn, wq_ref[...]) * (HEAD_DIM ** -0.5 * LOG2E)).astype(q_ref.dtype)
    g_ref[...] = jax.nn.sigmoid(_dot(xn, wg_ref[...]))
    return _dot(kn, wkv_ref[...])


def _l1_in_prompt_kernel(h_ref, nw_ref, kvnw_ref, wq_ref, wg_ref, wkv_ref,
                         q_ref, g_ref, cmpr_ref, cmpt_ref, slct_ref, wint_ref, kvtb_ref):
    kv = _l1_proj(h_ref, nw_ref, kvnw_ref, wq_ref, wg_ref, wkv_ref, q_ref, g_ref)
    cmpr_ref[...] = kv[:, 0:KV_ROW]
    kvt = kv.T
    cmpt_ref[...] = kvt[0:KV_ROW, :]
    slct_ref[...] = kvt[KV_ROW:2 * KV_ROW, :]
    wint_ref[...] = kvt[2 * KV_ROW:3 * KV_ROW, :]
    kvtb_ref[...] = kvt[KV_ROW:3 * KV_ROW, :].astype(BF16)


def _l1_in_sample_kernel(h_ref, nw_ref, kvnw_ref, wq_ref, wg_ref, wkv_ref, q_ref, g_ref, cmp_ref, slc_ref, win_ref):
    kv = _l1_proj(h_ref, nw_ref, kvnw_ref, wq_ref, wg_ref, wkv_ref, q_ref, g_ref)
    cmp_ref[...] = kv[:, 0:KV_ROW]
    slc_ref[...] = kv[:, KV_ROW:2 * KV_ROW]
    win_ref[...] = kv[:, 2 * KV_ROW:3 * KV_ROW]


def _l1_in_prompt(h, nw, kvnw, wq, wg, wkv):
    n, t, d = h.shape
    tm = ROW_TILE
    nq, ng = wq.shape[1], wg.shape[1]
    row = lambda w: pl.BlockSpec((None, tm, w), lambda b, i: (b, i, 0))
    col = lambda r: pl.BlockSpec((None, r, tm), lambda b, i: (b, 0, i))
    full = lambda a: pl.BlockSpec(a.shape, lambda b, i: (0, 0))
    tshape = lambda r, dt: jax.ShapeDtypeStruct((n, r, t), dt)
    return pl.pallas_call(
        _l1_in_prompt_kernel,
        grid=(n, t // tm),
        in_specs=[row(d), full(nw), full(kvnw), full(wq), full(wg), full(wkv)],
        out_specs=[row(nq), row(ng), row(KV_ROW), col(KV_ROW), col(KV_ROW), col(KV_ROW), col(2 * KV_ROW)],
        out_shape=[jax.ShapeDtypeStruct((n, t, nq), BF16), jax.ShapeDtypeStruct((n, t, ng), F32),
                   jax.ShapeDtypeStruct((n, t, KV_ROW), F32),
                   tshape(KV_ROW, F32), tshape(KV_ROW, F32), tshape(KV_ROW, F32), tshape(2 * KV_ROW, BF16)],
        compiler_params=_params("arbitrary", "arbitrary"),
        name="l1_in_prompt",
    )(h, nw, kvnw, wq, wg, wkv)


def _l1_in_sample(h, nw, kvnw, wq, wg, wkv):
    m, d = h.shape
    tm = ROW_TILE
    nq, ng = wq.shape[1], wg.shape[1]
    row = lambda w: pl.BlockSpec((tm, w), lambda i: (i, 0))
    full = lambda a: pl.BlockSpec(a.shape, lambda i: (0, 0))
    return pl.pallas_call(
        _l1_in_sample_kernel,
        grid=(m // tm,),
        in_specs=[row(d), full(nw), full(kvnw), full(wq), full(wg), full(wkv)],
        out_specs=[row(nq), row(ng), row(KV_ROW), row(KV_ROW), row(KV_ROW)],
        out_shape=[jax.ShapeDtypeStruct((m, nq), F32), jax.ShapeDtypeStruct((m, ng), F32),
                   jax.ShapeDtypeStruct((m, KV_ROW), F32), jax.ShapeDtypeStruct((m, KV_ROW), F32),
                   jax.ShapeDtypeStruct((m, KV_ROW), F32)],
        compiler_params=_params("arbitrary"),
        name="l1_in_sample",
    )(h, nw, kvnw, wq, wg, wkv)


def _compress_tokens(gather, tc, pe_ref, w1d_ref, w1p_ref, w2_ref, shift_ref):
    pieces = []
    for p in range(KV_ROW // LANES):
        k = p // 2
        xp = jnp.concatenate([gather(p, j) for j in range(CMP_STRIDE)], axis=1).astype(BF16)
        part = _dot(xp, w1p_ref[k])
        shift_ref[...] = part[:, LANES:2 * LANES]
        peb = _dot(pe_ref[k].astype(BF16), w1d_ref[k])
        pre = part[0:tc, 0:LANES] + shift_ref[pl.ds(1, tc), :] + peb
        pieces.append(_dot(_silu(pre).astype(BF16), w2_ref[k]))
    return jnp.concatenate(pieces, axis=1)


def _compress_prompt_kernel(x_ref, halo_ref, pe_ref, w1d_ref, w1p_ref, w2_ref, kc_ref, kct_ref, xs_ref, shift_ref):
    tc = x_ref.shape[0]
    xs_ref[0:tc, :] = x_ref[...]
    xs_ref[tc:tc + SUBLANES, :] = halo_ref[...]
    gather = lambda p, j: xs_ref[:, (j * 4 + p) * LANES:(j * 4 + p + 1) * LANES]
    kc = _compress_tokens(gather, tc, pe_ref, w1d_ref, w1p_ref, w2_ref, shift_ref)
    kc_ref[...] = kc.astype(kc_ref.dtype)
    kct_ref[...] = kc.T.astype(kct_ref.dtype)


def _compress_prompt(chunks, pe, w1d, w1p, w2):
    n, c, _ = chunks.shape
    tc = CMP_TILE
    nhalo = c // SUBLANES
    full = lambda a: pl.BlockSpec(a.shape, lambda b, i: (0,) * a.ndim)
    return pl.pallas_call(
        _compress_prompt_kernel,
        grid=(n, c // tc),
        in_specs=[
            pl.BlockSpec((None, tc, CHUNK_LANES), lambda b, i: (b, i, 0)),
            pl.BlockSpec((None, SUBLANES, CHUNK_LANES),
                         lambda b, i: (b, jnp.minimum((i + 1) * (tc // SUBLANES), nhalo - 1), 0)),
            full(pe), full(w1d), full(w1p), full(w2),
        ],
        out_specs=[pl.BlockSpec((None, tc, KV_ROW), lambda b, i: (b, i, 0)),
                   pl.BlockSpec((None, KV_ROW, tc), lambda b, i: (b, 0, i))],
        out_shape=[jax.ShapeDtypeStruct((n, c, KV_ROW), BF16), jax.ShapeDtypeStruct((n, KV_ROW, c), BF16)],
        scratch_shapes=[pltpu.VMEM((tc + SUBLANES, CHUNK_LANES), F32), pltpu.VMEM((tc + SUBLANES, LANES), F32)],
        compiler_params=_params("arbitrary", "arbitrary"),
        name="compress_prompt",
    )(chunks, chunks, pe, w1d, w1p, w2)


def _compress_sample_kernel(pt_ref, new_ref, pe_ref, w1d_ref, w1p_ref, w2_ref, *rest, n_pages, bps):
    page_refs = rest[:bps * n_pages]
    kc_ref, xs_ref, shift_ref = rest[bps * n_pages:]
    for bi in range(bps):
        _compress_sample_one(new_ref.at[bi], pe_ref, w1d_ref, w1p_ref, w2_ref,
                             page_refs[bi * n_pages:(bi + 1) * n_pages], kc_ref.at[bi], xs_ref.at[bi], shift_ref.at[bi])


def _compress_sample_one(new_ref, pe_ref, w1d_ref, w1p_ref, w2_ref, page_refs, kc_ref, xs_ref, shift_ref):
    n_pages = len(page_refs)
    n_pieces = KV_ROW // LANES
    past = n_pages * PAGE_SIZE
    seq = new_ref.shape[0]
    tc = past // CMP_STRIDE
    per_page = PAGE_SIZE // CMP_STRIDE
    pitch = tc + SUBLANES
    for p in range(n_pieces):
        for j in range(CMP_STRIDE):
            xs_ref[p, j * pitch + tc:(j + 1) * pitch, :] = jnp.zeros((SUBLANES, LANES), F32)
        xs_ref[p, pl.ds(tc, seq, stride=pitch), :] = new_ref[:, p * LANES:(p + 1) * LANES]
    for pg in range(n_pages):
        page = page_refs[pg][...]
        for p in range(n_pieces):
            tile = page[p * LANES:(p + 1) * LANES, :].T
            for v in range(PAGE_SIZE // SUBLANES):
                c, j0 = divmod(v * SUBLANES, CMP_STRIDE)
                xs_ref[p, pl.ds(j0 * pitch + pg * per_page + c, SUBLANES, stride=pitch), :] = (
                    tile[v * SUBLANES:(v + 1) * SUBLANES, :])
    gather = lambda p, j: xs_ref[p, j * pitch:(j + 1) * pitch, :]
    kc = _compress_tokens(gather, tc, pe_ref, w1d_ref, w1p_ref, w2_ref, shift_ref)
    kc_ref[...] = kc.astype(kc_ref.dtype)


def _compress_sample(page_table, cache_t, new_rows, pe, w1d, w1p, w2):
    nb, seq, _ = new_rows.shape
    n_pages = page_table.shape[0] // nb
    bps = SAMPLE_BATCH_PER_STEP
    tc = n_pages * PAGE_SIZE // CMP_STRIDE
    full = lambda a: pl.BlockSpec(a.shape, lambda b, pt: (0,) * a.ndim)
    page_specs = [pl.BlockSpec((None, KV_ROW, PAGE_SIZE),
                               lambda b, pt, bi=bi, p=p: (pt[(b * bps + bi) * n_pages + p], 0, 0))
                  for bi in range(bps) for p in range(n_pages)]
    return pl.pallas_call(
        functools.partial(_compress_sample_kernel, n_pages=n_pages, bps=bps),
        grid_spec=pltpu.PrefetchScalarGridSpec(
            num_scalar_prefetch=1,
            grid=(nb // bps,),
            in_specs=[pl.BlockSpec((bps, seq, KV_ROW), lambda b, pt: (b, 0, 0)),
                      full(pe), full(w1d), full(w1p), full(w2)] + page_specs,
            out_specs=pl.BlockSpec((bps, tc, KV_ROW), lambda b, pt: (b, 0, 0)),
            scratch_shapes=[pltpu.VMEM((bps, KV_ROW // LANES, (tc + SUBLANES) * CMP_STRIDE, LANES), F32),
                            pltpu.VMEM((bps, tc + SUBLANES, LANES), F32)],
        ),
        out_shape=jax.ShapeDtypeStruct((nb, tc, KV_ROW), BF16),
        compiler_params=_params("arbitrary"),
        name="compress_sample",
    )(page_table, new_rows, pe, w1d, w1p, w2, *([cache_t] * (bps * n_pages)))


def _top_n_rows(score, n_pick):
    rows = lax.broadcasted_iota(jnp.int32, score.shape, 0).astype(F32)
    sel = jnp.zeros(score.shape, F32)
    s = score
    for _ in range(n_pick):
        m = jnp.max(s, axis=0, keepdims=True)
        first = jnp.min(jnp.where(s == m, rows, float(score.shape[0])), axis=0, keepdims=True)
        hit = rows == first
        sel = jnp.where(hit, 1.0, sel)
        s = jnp.where(hit, -jnp.inf, s)
    return sel


def _softmax_cols(s, mask):
    s = jnp.where(mask, s, NEG)
    m = jnp.max(s, axis=0, keepdims=True)
    e = jnp.where(mask, jnp.exp2(s - m), 0.0)
    return e * (1.0 / jnp.maximum(jnp.sum(e, axis=0, keepdims=True), 1e-30))


def _softmax_rows_unnorm(s, mask):
    s = jnp.where(mask, s, NEG)
    m = jnp.max(s, axis=1, keepdims=True)
    e = jnp.where(mask, jnp.exp2(s - m), 0.0)
    return e, 1.0 / jnp.maximum(jnp.sum(e, axis=1, keepdims=True), 1e-30)


def _edge_biases(tq):
    t = lax.broadcasted_iota(jnp.int32, (tq, tq), 0)
    i = lax.broadcasted_iota(jnp.int32, (tq, tq), 1)
    return jnp.where(i <= t, 0.0, NEG), jnp.where(i >= t, 0.0, NEG)


def _block_bias(d, causal, lower, lower_at):
    bias = jnp.where(d == 0, causal, jnp.where(d < 0, NEG, 0.0))
    if lower_at is not None:
        bias = jnp.where(d == lower_at, lower, bias)
    return bias


def _add_per_head(s, bias, n_heads):
    tq = bias.shape[0]
    return jnp.concatenate([s[h * tq:(h + 1) * tq] + bias for h in range(n_heads)], axis=0)


def _block_scores(imp, pos, n_sel):
    jj = lax.broadcasted_iota(jnp.int32, imp.shape, 0)
    valid = (jj * SEL_BLK <= pos) & (jj < n_sel)
    cur = pos // SEL_BLK
    forced = (jj == 0) | (jj == cur) | (jj == cur - 1)
    score = jnp.where(valid & forced, BIG, jnp.where(valid, imp, -BIG))
    return jnp.where(jj < n_sel, score, -jnp.inf)


def _split_dot(w_bf16, x):
    hi = x.astype(BF16)
    lo = (x - hi.astype(F32)).astype(BF16)
    return _dot(w_bf16, hi) + _dot(w_bf16, lo)


def _with_ones(v_half, ones, first):
    return jnp.concatenate([v_half, ones] if first else [ones, v_half], axis=0)


def _norm_by_other_half(acc):
    return acc * (1.0 / jnp.maximum(pltpu.roll(acc, HEAD_DIM, axis=1), 1e-30))


def _attn_prompt_kernel(q_ref, g_ref, kc_ref, vct_ref, skt_ref, svt_ref, wkt_ref, wvt_ref, et_ref, aggt_ref, z_ref):
    tq = q_ref.shape[0]
    tkc = KEY_CHUNK
    qt = pl.program_id(2)
    t0 = qt * tq
    heads = 2 * GROUP
    r_rows = heads * tq
    n_cmp = kc_ref.shape[0]
    n_sel = et_ref.shape[1] // SEL_BLK
    half = lax.broadcasted_iota(jnp.int32, (tq, LANES), 1) // HEAD_DIM
    qpos_row = t0 + (lax.broadcasted_iota(jnp.int32, (1, r_rows), 1) & (tq - 1))

    qs = [q_ref[:, h * LANES:(h + 1) * LANES] for h in range(heads)]
    qall = jnp.concatenate(qs, axis=0)

    causal_b, lower_b = _edge_biases(tq)

    def pv_ones(p, v, ones_rows):
        return jnp.concatenate(
            [_dot_nt(p[gg * GROUP * tq:(gg + 1) * GROUP * tq],
                     _with_ones(v[gg * HEAD_DIM:(gg + 1) * HEAD_DIM], ones_rows, gg == 0)) for gg in range(2)], axis=0)

    cend = lax.broadcasted_iota(jnp.int32, (n_cmp, 1), 0) * CMP_STRIDE + (CMP_BLK - 1)
    pt = _softmax_cols(_dot_nt(kc_ref[...], qall), cend <= qpos_row)
    psum = []
    for gg in range(2):
        acc = pt[:, gg * GROUP * tq:(gg * GROUP + 1) * tq]
        for r in range(1, GROUP):
            acc = acc + pt[:, (gg * GROUP + r) * tq:(gg * GROUP + r + 1) * tq]
        psum.append(acc)
    imp = _split_dot(aggt_ref[...], jnp.concatenate(psum, axis=1))
    oct = _dot(vct_ref[...], pt.astype(BF16))

    n_wblk = WINDOW // tq + 1
    start = pl.multiple_of(jnp.maximum(t0 - WINDOW, 0), LANES)
    back0 = jnp.minimum(WINDOW // tq, qt)
    sw = jnp.concatenate(
        [_add_per_head(_dot(qall, wkt_ref[:, pl.ds(pl.multiple_of(start + b * tq, LANES), tq)]),
                       _block_bias(back0 - b, causal_b, lower_b, WINDOW // tq), heads) for b in range(n_wblk)],
        axis=1).astype(BF16)
    ew = jnp.exp2(sw - jnp.max(sw, axis=1, keepdims=True))
    o_w = _norm_by_other_half(pv_ones(ew, wvt_ref[:, pl.ds(start, n_wblk * tq)], jnp.ones((HEAD_DIM, n_wblk * tq), BF16)))

    gates = g_ref[...]
    part = [gates[:, 3 * h:3 * h + 1] * oct[:, h * tq:(h + 1) * tq].T
            + gates[:, 3 * h + 2:3 * h + 3] * o_w[h * tq:(h + 1) * tq, :] for h in range(heads)]

    pos = t0 + (lax.broadcasted_iota(jnp.int32, imp.shape, 1) & (tq - 1))
    selt = _top_n_rows(_block_scores(imp, pos, n_sel), min(TOP_N, n_sel))
    selm1 = [(selt[:, gg * tq:(gg + 1) * tq].T - 1.0).astype(BF16) for gg in range(2)]
    qaug = jnp.concatenate([jnp.concatenate([qs[h], selm1[h // GROUP]], axis=1) for h in range(heads)], axis=0)

    def sel_chunk(c, width, carry, diag_block):
        m_i, acc = carry
        keys = pl.ds(pl.multiple_of(c * width, width), width)
        rhs = jnp.concatenate([skt_ref[:, keys], et_ref[:, keys]], axis=0)
        s = _dot(qaug, rhs)
        if diag_block is not None:
            s = jnp.concatenate(
                [_add_per_head(s[:, b * tq:(b + 1) * tq], _block_bias(diag_block - b, causal_b, lower_b, None), heads)
                 for b in range(width // tq)], axis=1)
        sb = s.astype(BF16)
        m_new = jnp.maximum(m_i, jnp.max(sb, axis=1, keepdims=True).astype(F32))
        alpha = jnp.exp2(m_i - m_new)
        p = jnp.exp2(sb - m_new.astype(BF16))
        return m_new, alpha * acc + pv_ones(p, svt_ref[:, keys], jnp.ones((HEAD_DIM, width), BF16))

    n_full = t0 // tkc
    n_wide = n_full // 2
    init = (jnp.full((r_rows, 1), M_INIT, F32), jnp.zeros((r_rows, LANES), F32))
    carry = lax.fori_loop(0, n_wide, lambda c, cr: sel_chunk(c, 2 * tkc, cr, None), init)
    carry = lax.fori_loop(2 * n_wide, n_full, lambda c, cr: sel_chunk(c, tkc, cr, None), carry)
    _, acc_s = sel_chunk(n_full, tkc, carry, qt - n_full * (tkc // tq))
    o_s = _norm_by_other_half(acc_s)

    head_out = [part[h] + gates[:, 3 * h + 1:3 * h + 2] * o_s[h * tq:(h + 1) * tq, :] for h in range(heads)]
    for i in range(heads // 2):
        gg = (2 * i) // GROUP
        a, b = head_out[2 * i], head_out[2 * i + 1]
        packed = (jnp.where(half == 0, a, pltpu.roll(b, HEAD_DIM, axis=1)) if gg == 0
                  else jnp.where(half == 0, pltpu.roll(a, HEAD_DIM, axis=1), b))
        z_ref[:, i * LANES:(i + 1) * LANES] = packed.astype(z_ref.dtype)


def _attn_prompt(q2, gates2, kvc, kvct, kvtb, e_t, aggt):
    n, t, _ = q2.shape
    tq = Q_TILE
    npair = N_KV_HEADS // 2
    pair_w = 2 * GROUP * LANES
    kv_slab = lambda branch, kv: pl.BlockSpec(
        (None, LANES, t), lambda b, gp, i: (b, branch * 2 * npair + kv * npair + gp, 0))
    full = lambda a: pl.BlockSpec(a.shape, lambda b, gp, i: (0,) * a.ndim)
    return pl.pallas_call(
        _attn_prompt_kernel,
        grid=(n, npair, t // tq),
        in_specs=[
            pl.BlockSpec((None, tq, pair_w), lambda b, gp, i: (b, i, gp)),
            pl.BlockSpec((None, tq, LANES), lambda b, gp, i: (b, i, gp)),
            pl.BlockSpec((None, kvc.shape[1], LANES), lambda b, gp, i: (b, 0, gp)),
            pl.BlockSpec((None, LANES, kvct.shape[2]), lambda b, gp, i: (b, npair + gp, 0)),
            kv_slab(0, 0), kv_slab(0, 1), kv_slab(1, 0), kv_slab(1, 1),
            full(e_t), full(aggt),
        ],
        out_specs=pl.BlockSpec((None, tq, 2 * GROUP * HEAD_DIM), lambda b, gp, i: (b, i, gp)),
        out_shape=jax.ShapeDtypeStruct((n, t, N_HEADS * HEAD_DIM), BF16),
        compiler_params=_params("arbitrary", "arbitrary", "arbitrary"),
        name="attn_prompt",
    )(q2, gates2, kvc, kvct, kvtb, kvtb, kvtb, kvtb, e_t, aggt)


def _attn_sample_kernel(pt_ref, qbd_ref, g_ref, kvc_ref, nslc_ref, wst_ref, nwin_ref, et_ref, enew_ref, aggt_ref,
                        *rest, n_pages, past_len, bps):
    page_refs = rest[:bps * n_pages]
    z_ref, wout_ref = rest[bps * n_pages:]
    seq = qbd_ref.shape[1]
    rows = GROUP * N_KV_HEADS * seq
    n_sel = -(-(n_pages * PAGE_SIZE + seq) // SEL_BLK)
    front = [_attn_sample_front(qbd_ref.at[bi], kvc_ref.at[bi], aggt_ref, past_len, n_sel) for bi in range(bps)]
    selt = _top_n_rows(jnp.concatenate([f[2] for f in front], axis=1), min(TOP_N, n_sel))
    for bi in range(bps):
        _attn_sample_back(front[bi][0], front[bi][1], selt[:, bi * rows:(bi + 1) * rows], g_ref.at[bi],
                          nslc_ref.at[bi], wst_ref.at[bi], nwin_ref.at[bi], et_ref, enew_ref,
                          page_refs[bi * n_pages:(bi + 1) * n_pages], z_ref.at[bi], wout_ref.at[bi], past_len)


def _attn_sample_front(qbd_ref, kvc_ref, aggt_ref, past_len, n_sel):
    seq = qbd_ref.shape[0]
    kw = KV_HALF
    rows = GROUP * N_KV_HEADS * seq
    n_cmp = kvc_ref.shape[0]

    qbd = jnp.concatenate(
        [qbd_ref[:, (g * GROUP + r) * kw:(g * GROUP + r + 1) * kw] for r in range(GROUP) for g in range(N_KV_HEADS)],
        axis=0).astype(BF16)
    qpos_row = past_len + (lax.broadcasted_iota(jnp.int32, (1, rows), 1) & (seq - 1))

    cend = lax.broadcasted_iota(jnp.int32, (n_cmp, 1), 0) * CMP_STRIDE + (CMP_BLK - 1)
    pt = _softmax_cols(_dot_nt(kvc_ref[:, 0:kw], qbd), cend <= qpos_row)
    imp = _split_dot(aggt_ref[...], pt)
    per_r = N_KV_HEADS * seq
    imp_g = imp
    for r in range(1, GROUP):
        imp_g = imp_g + pltpu.roll(imp, r * per_r, axis=1)
    vct = kvc_ref[:, kw:2 * kw].astype(F32).T.astype(BF16)
    o_c = _dot(vct, pt.astype(BF16)).T
    return qbd, o_c, _block_scores(imp_g, jnp.broadcast_to(qpos_row, imp_g.shape), n_sel)


def _attn_sample_back(qbd, o_c, selt, g_ref, nslc_ref, wst_ref, nwin_ref, et_ref, enew_ref, page_refs,
                      z_ref, wout_ref, past_len):
    n_pages = len(page_refs)
    seq = g_ref.shape[0]
    kw = KV_HALF
    rows = GROUP * N_KV_HEADS * seq
    w_len = wst_ref.shape[1]
    qpos_col = past_len + (lax.broadcasted_iota(jnp.int32, (rows, 1), 0) & (seq - 1))
    selm1 = (selt.T - 1.0).astype(BF16)
    qaug = jnp.concatenate([qbd, selm1], axis=1)

    pad = jnp.zeros((PAGE_SIZE - seq, kw), F32)
    new_tile = lambda ref, lo: jnp.concatenate([ref[:, lo:lo + kw], pad], axis=0).astype(BF16)

    s_tiles = []
    for p in range(n_pages):
        rhs = jnp.concatenate([page_refs[p][0:kw, :].astype(BF16), et_ref[:, p * PAGE_SIZE:(p + 1) * PAGE_SIZE]], axis=0)
        s_tiles.append(_dot(qaug, rhs))
    s_tiles.append(_dot_nt(qaug, jnp.concatenate([new_tile(nslc_ref, 0), enew_ref[...]], axis=1)))
    s_all = jnp.concatenate(s_tiles, axis=1)
    kpos = lax.broadcasted_iota(jnp.int32, (1, s_all.shape[1]), 1)
    es, inv_s = _softmax_rows_unnorm(s_all, kpos <= qpos_col)
    es = es.astype(BF16)
    o_s = _dot(es[:, n_pages * PAGE_SIZE:], new_tile(nslc_ref, kw))
    for p in range(n_pages):
        o_s = o_s + _dot_nt(es[:, p * PAGE_SIZE:(p + 1) * PAGE_SIZE], page_refs[p][kw:2 * kw, :].astype(BF16))
    o_s = o_s * inv_s

    wst = wst_ref[...]
    sw = jnp.concatenate([_dot(qbd, wst[0:kw, :].astype(BF16)), _dot_nt(qbd, new_tile(nwin_ref, 0))], axis=1)
    kpos_w = past_len - w_len + lax.broadcasted_iota(jnp.int32, (1, w_len + PAGE_SIZE), 1)
    mw = (kpos_w <= qpos_col) & (kpos_w >= qpos_col - WINDOW) & (kpos_w >= 0)
    ew, inv_w = _softmax_rows_unnorm(sw, mw)
    ew = ew.astype(BF16)
    o_w = (_dot_nt(ew[:, 0:w_len], wst[kw:2 * kw, :].astype(BF16)) + _dot(ew[:, w_len:], new_tile(nwin_ref, kw))) * inv_w

    gates = g_ref[...]
    half = lax.broadcasted_iota(jnp.int32, (seq, LANES), 1) // HEAD_DIM

    def head_out(g, r):
        rs = slice((r * N_KV_HEADS + g) * seq, (r * N_KV_HEADS + g + 1) * seq)
        cs = slice((g // 2) * LANES, (g // 2 + 1) * LANES)
        c = (g * GROUP + r) * 3
        return (gates[:, c:c + 1] * o_c[rs, cs] + gates[:, c + 1:c + 2] * o_s[rs, cs]
                + gates[:, c + 2:c + 3] * o_w[rs, cs])

    for g in range(N_KV_HEADS):
        for i in range(GROUP // 2):
            a, b = head_out(g, 2 * i), head_out(g, 2 * i + 1)
            packed = (jnp.where(half == 0, a, pltpu.roll(b, HEAD_DIM, axis=1)) if g % 2 == 0
                      else jnp.where(half == 0, pltpu.roll(a, HEAD_DIM, axis=1), b))
            col = g * (GROUP // 2) + i
            z_ref[:, col * LANES:(col + 1) * LANES] = packed

    new_t = jnp.concatenate([nwin_ref[...], jnp.zeros((LANES - seq, KV_ROW), F32)], axis=0).T
    shifted = pltpu.roll(wst, w_len - seq, axis=1)
    lane = lax.broadcasted_iota(jnp.int32, (KV_ROW, LANES), 1)
    wout_ref[:, 0:w_len - LANES] = shifted[:, 0:w_len - LANES]
    wout_ref[:, w_len - LANES:w_len] = jnp.where(lane >= LANES - seq, pltpu.roll(new_t, LANES - seq, axis=1),
                                                 shifted[:, w_len - LANES:w_len])


def _attn_sample(page_table, qbd, gates, kvc, new_slc, cache_t, win_t, new_win, e_t, e_new, aggt, past_len):
    nb, seq, _ = qbd.shape
    n_pages = page_table.shape[0] // nb
    w_len = win_t.shape[2]
    assert w_len == min(WINDOW, past_len + seq) and w_len % LANES == 0
    bps = SAMPLE_BATCH_PER_STEP
    full = lambda a: pl.BlockSpec(a.shape, lambda b, pt: (0,) * a.ndim)
    per_b = lambda a: pl.BlockSpec((bps,) + a.shape[1:], lambda b, pt: (b,) + (0,) * (a.ndim - 1))
    page_specs = [pl.BlockSpec((None, KV_ROW, PAGE_SIZE),
                               lambda b, pt, bi=bi, p=p: (pt[(b * bps + bi) * n_pages + p], 0, 0))
                  for bi in range(bps) for p in range(n_pages)]
    zw = N_HEADS * HEAD_DIM
    return pl.pallas_call(
        functools.partial(_attn_sample_kernel, n_pages=n_pages, past_len=past_len, bps=bps),
        grid_spec=pltpu.PrefetchScalarGridSpec(
            num_scalar_prefetch=1,
            grid=(nb // bps,),
            in_specs=[per_b(qbd), per_b(gates), per_b(kvc), per_b(new_slc), per_b(win_t), per_b(new_win),
                      full(e_t), full(e_new), full(aggt)] + page_specs,
            out_specs=[pl.BlockSpec((bps, seq, zw), lambda b, pt: (b, 0, 0)),
                       pl.BlockSpec((bps, KV_ROW, w_len), lambda b, pt: (b, 0, 0))],
        ),
        out_shape=[jax.ShapeDtypeStruct((nb, seq, zw), F32), jax.ShapeDtypeStruct((nb, KV_ROW, w_len), F32)],
        compiler_params=_params("arbitrary"),
        name="attn_sample",
    )(page_table, qbd, gates, kvc, new_slc, win_t, new_win, e_t, e_new, aggt, *([cache_t] * (bps * n_pages)))


def _pad_cols(w, width):
    return jnp.pad(w, ((0, 0), (0, width - w.shape[1])))


def _pack_weights(b_in_w, cmp_w1, cmp_w2):
    d = b_in_w.shape[0]
    nq = N_HEADS * HEAD_DIM
    wq = b_in_w[:, :nq].reshape(d, N_KV_HEADS, GROUP, HEAD_DIM)
    wgate = b_in_w[:, nq:]
    eye2 = jnp.eye(2, dtype=F32)
    eye4 = jnp.eye(N_KV_HEADS, dtype=F32)
    par = jnp.arange(N_KV_HEADS) % 2
    sel2 = jax.nn.one_hot(par, 2, dtype=F32)
    wq_p = jnp.einsum('dgrh,ga->dgrah', wq, sel2).reshape(d, N_HEADS * LANES)
    wq_s = jnp.einsum('dgrh,ga->dgrah', wq, eye4).reshape(d, N_HEADS * N_KV_HEADS * HEAD_DIM)
    per_pair = 2 * GROUP * 3
    wg_p = jnp.concatenate([_pad_cols(wgate[:, i * per_pair:(i + 1) * per_pair], LANES)
                            for i in range(N_KV_HEADS // 2)], axis=1)
    wg_s = _pad_cols(wgate, LANES)
    w1 = cmp_w1.reshape(2, CMP_BLK // CMP_STRIDE, CMP_STRIDE, HEAD_DIM, -1)
    hid = w1.shape[-1]
    w1p = jnp.einsum('krjdh,ab->kjadrbh', w1, eye2).reshape(2, CMP_STRIDE * LANES, 2 * 2 * hid)
    w1d = jnp.concatenate([cmp_w1.reshape(2, CMP_BLK * HEAD_DIM, hid)] * 2, axis=2)
    w2p = jnp.einsum('khe,ab->kahbe', cmp_w2, eye2).reshape(2, 2 * hid, 2 * HEAD_DIM)
    cast = lambda a: a.astype(BF16)
    return tuple(map(cast, (wq_p, wq_s, wg_p, wg_s, w1p, w1d, w2p)))


def _agg_t(n_cmp, n_sel_pad):
    c0 = jnp.arange(n_cmp) * CMP_STRIDE
    s0 = jnp.arange(n_sel_pad) * SEL_BLK
    ov = jnp.clip(jnp.minimum(c0[None, :] + CMP_BLK, s0[:, None] + SEL_BLK) - jnp.maximum(c0[None, :], s0[:, None]),
                  0, None)
    return (ov.astype(F32) / CMP_STRIDE).astype(BF16)


def _block_onehot(key0, n_keys):
    blk = (key0 + jnp.arange(n_keys)) // SEL_BLK
    return (jax.nn.one_hot(blk, LANES, dtype=F32) * BIG).astype(BF16)


def _feature_major(a):
    return a.transpose(0, 2, 3, 4, 1).reshape(a.shape[0], KV_ROW, a.shape[1])


def _position_major(a_t):
    b, _, pos = a_t.shape
    return a_t.reshape(b, 2, N_KV_HEADS, HEAD_DIM, pos).transpose(0, 4, 1, 2, 3)


def kernel(x_prompt, x_sample, cache_cmp_kv, cache_slc_kv, state_win_kv, state_conv, page_table, norm_w, final_norm_w, a_in_w, a_conv_w, a_out_w, b_in_w, b_out_w, kv_norm_w, kv_w, cmp_pe, cmp_w1, cmp_w2, ffn_in_w, ffn_out_w):
    n, t, d = x_prompt.shape
    nb, ts, _ = x_sample.shape
    n_pages = page_table.shape[1]
    past_len = n_pages * PAGE_SIZE
    d_ff = ffn_out_w.shape[1]
    assert a_in_w.shape[0] == 1 and b_in_w.shape[0] == 1 and d == N_HEADS * HEAD_DIM
    assert t % (CMP_TILE * CMP_STRIDE) == 0 and t % KEY_CHUNK == 0 and KEY_CHUNK % Q_TILE == 0
    assert ts == SUBLANES and (nb * ts) % ROW_TILE == 0 and d_ff % (2 * LANES) == 0
    assert nb % SAMPLE_BATCH_PER_STEP == 0 and WINDOW % Q_TILE == 0

    wq_p, wq_s, wg_p, wg_s, w1p, w1d, w2p = _pack_weights(b_in_w[0], cmp_w1, cmp_w2)
    a_in, a_out, kvw = a_in_w[0].astype(BF16), a_out_w[0].astype(BF16), kv_w.astype(BF16)
    b_out = b_out_w[0].astype(BF16)
    ffn_in, ffn_out = ffn_in_w.astype(BF16), ffn_out_w.astype(BF16)
    nw = norm_w.reshape(norm_w.shape[0], 2, 1, d)
    fnw = final_norm_w.reshape(1, d)
    kvnw = kv_norm_w.reshape(1, d)
    pe = cmp_pe.reshape(2, 1, CMP_BLK * HEAD_DIM)
    mp, ms = n * t, nb * ts

    z0, tail_p = _l0_in_prompt(x_prompt, nw[0, 0], a_in, a_conv_w[0])
    h1 = _mix_ffn(x_prompt.reshape(mp, d), z0.reshape(mp, d), a_out, nw[0, 1], ffn_in[0], ffn_out[0], fnw, final=False)
    q2, g2, cmp_rows, cmp_t, slc_t, win_t, kvtb = _l1_in_prompt(h1.reshape(n, t, d), nw[1, 0], kvnw, wq_p, wg_p, kvw)
    n_chunks = t // CMP_STRIDE
    kvc, kvct = _compress_prompt(cmp_rows.reshape(n, n_chunks, CHUNK_LANES), pe, w1d, w1p, w2p)
    z1 = _attn_prompt(q2, g2, kvc, kvct, kvtb, _block_onehot(0, t).T, _agg_t(n_chunks, LANES))
    y_p = _mix_ffn(h1, z1.reshape(mp, -1), b_out, nw[1, 1], ffn_in[1], ffn_out[1], fnw, final=True)

    zero = jnp.zeros((nb, ts - 2, d), F32)
    pre1 = jnp.concatenate([state_conv[0, :, 1:2], jnp.zeros((nb, ts - 1, d), F32)], axis=1).reshape(ms, d)
    pre2 = jnp.concatenate([state_conv[0], zero], axis=1).reshape(ms, d)
    xs = x_sample.reshape(ms, d)
    z0s, v_s = _l0_in_sample(xs, pre1, pre2, nw[0, 0], a_in, a_conv_w[0], seq=ts)
    h1s = _mix_ffn(xs, z0s, a_out, nw[0, 1], ffn_in[0], ffn_out[0], fnw, final=False)
    qbd, gs, cmp_s, slc_s, win_s_new = _l1_in_sample(h1s, nw[1, 0], kvnw, wq_s, wg_s, kvw)
    pt_flat = page_table.reshape(-1)
    kvc_s = _compress_sample(pt_flat, _feature_major(cache_cmp_kv), cmp_s.reshape(nb, ts, KV_ROW), pe, w1d, w1p, w2p)
    z1s, win_s_t = _attn_sample(pt_flat, qbd.reshape(nb, ts, -1), gs.reshape(nb, ts, -1), kvc_s,
                                slc_s.reshape(nb, ts, KV_ROW), _feature_major(cache_slc_kv),
                                _feature_major(state_win_kv), win_s_new.reshape(nb, ts, KV_ROW),
                                _block_onehot(0, past_len).T, _block_onehot(past_len, PAGE_SIZE),
                                _agg_t(kvc_s.shape[1], LANES), past_len)
    y_s = _mix_ffn(h1s, z1s.reshape(ms, -1), b_out, nw[1, 1], ffn_in[1], ffn_out[1], fnw, final=True)

    kv_shape = (2, N_KV_HEADS, HEAD_DIM)
    keep_p = min(WINDOW, t)
    return (
        y_p.reshape(n, t, d),
        y_s.reshape(nb, ts, d),
        tail_p[:, SUBLANES - (CONV_W - 1):][None],
        v_s.reshape(nb, ts, d)[:, ts - (CONV_W - 1):][None],
        _position_major(cmp_t),
        cmp_s.reshape((nb, ts) + kv_shape),
        _position_major(slc_t),
        slc_s.reshape((nb, ts) + kv_shape),
        _position_major(win_t[:, :, t - keep_p:]),
        _position_major(win_s_t),
    )
```

```python
import functools

import jax
import jax.numpy as jnp
from jax import lax
from jax.experimental import pallas as pl
from jax.experimental.pallas import tpu as pltpu

F32 = jnp.float32
BF16 = jnp.bfloat16

HEAD_DIM = 64
N_HEADS = 16
N_KV_HEADS = 4
GROUP = N_HEADS // N_KV_HEADS
CMP_BLK = 32
CMP_STRIDE = 16
SEL_BLK = 64
TOP_N = 16
WINDOW = 512
PAGE_SIZE = 128
CONV_W = 3
RMS_EPS = 1e-6
BIG = 1e9
NEG = -1e30
LOG2E = 1.4426950408889634
M_INIT = -(2.0 ** 100)

LANES = 128
SUBLANES = 8
KV_ROW = 2 * N_KV_HEADS * HEAD_DIM
KV_HALF = KV_ROW // 2
CHUNK_LANES = CMP_STRIDE * KV_ROW
VMEM_LIMIT_BYTES = 56 * 1024 * 1024

ROW_TILE = 512
CMP_TILE = 128
Q_TILE = 256
KEY_CHUNK = 512
SAMPLE_BATCH_PER_STEP = 2

_NT = (((1,), (1,)), ((), ()))


def _dot(a, b):
    return jnp.dot(a, b, preferred_element_type=F32)


def _dot_nt(a, b):
    return lax.dot_general(a, b, _NT, preferred_element_type=F32)


def _rms_unit(x):
    return x * lax.rsqrt(jnp.mean(x * x, axis=-1, keepdims=True) + RMS_EPS)


def _silu(x):
    return x * jax.nn.sigmoid(x)


def _params(*sem):
    return pltpu.CompilerParams(dimension_semantics=sem, vmem_limit_bytes=VMEM_LIMIT_BYTES)


def _conv_gate(xn_bf16, win_ref, cw_ref, vm1_fix, vm2_fix, d):
    b = _dot(xn_bf16, win_ref[:, 0:d])
    c = _dot(xn_bf16, win_ref[:, d:2 * d])
    u = _dot(xn_bf16, win_ref[:, 2 * d:3 * d])
    v = c * u
    vm1 = vm1_fix(pltpu.roll(v, 1, axis=0))
    vm2 = vm2_fix(pltpu.roll(v, 2, axis=0))
    cw = cw_ref[...]
    conv = cw[0:1, :] * vm2 + cw[1:2, :] * vm1 + cw[2:3, :] * v
    return b * conv, v


def _l0_in_prompt_kernel(x_ref, nw_ref, win_ref, cw_ref, z_ref, tail_ref, carry_ref):
    i = pl.program_id(1)
    tm, d = x_ref.shape

    @pl.when(i == 0)
    def _():
        carry_ref[...] = jnp.zeros_like(carry_ref)

    xn = (_rms_unit(x_ref[...]) * nw_ref[...]).astype(BF16)
    prev = carry_ref[...]
    row = lax.broadcasted_iota(jnp.int32, (tm, 1), 0)
    fix1 = lambda r: jnp.where(row == 0, prev[7:8, :], r)
    fix2 = lambda r: jnp.where(row == 0, prev[6:7, :], jnp.where(row == 1, prev[7:8, :], r))
    z, v = _conv_gate(xn, win_ref, cw_ref, fix1, fix2, d)
    z_ref[...] = z.astype(z_ref.dtype)
    carry_ref[...] = v[tm - SUBLANES:tm, :]

    @pl.when(i == pl.num_programs(1) - 1)
    def _():
        tail_ref[...] = v[tm - SUBLANES:tm, :]


def _l0_in_sample_kernel(x_ref, p1_ref, p2_ref, nw_ref, win_ref, cw_ref, z_ref, v_ref, *, seq):
    tm, d = x_ref.shape
    xn = (_rms_unit(x_ref[...]) * nw_ref[...]).astype(BF16)
    t = lax.broadcasted_iota(jnp.int32, (tm, 1), 0) & (seq - 1)
    fix1 = lambda r: jnp.where(t == 0, p1_ref[...], r)
    fix2 = lambda r: jnp.where(t < 2, p2_ref[...], r)
    z, v = _conv_gate(xn, win_ref, cw_ref, fix1, fix2, d)
    z_ref[...] = z.astype(z_ref.dtype)
    v_ref[...] = v


def _l0_in_prompt(x, nw, w_in, conv_w):
    n, t, d = x.shape
    tm = ROW_TILE
    return pl.pallas_call(
        _l0_in_prompt_kernel,
        grid=(n, t // tm),
        in_specs=[
            pl.BlockSpec((None, tm, d), lambda b, i: (b, i, 0)),
            pl.BlockSpec((1, d), lambda b, i: (0, 0)),
            pl.BlockSpec((d, 3 * d), lambda b, i: (0, 0)),
            pl.BlockSpec((CONV_W, d), lambda b, i: (0, 0)),
        ],
        out_specs=[
            pl.BlockSpec((None, tm, d), lambda b, i: (b, i, 0)),
            pl.BlockSpec((None, SUBLANES, d), lambda b, i: (b, 0, 0)),
        ],
        out_shape=[jax.ShapeDtypeStruct((n, t, d), BF16), jax.ShapeDtypeStruct((n, SUBLANES, d), F32)],
        scratch_shapes=[pltpu.VMEM((SUBLANES, d), F32)],
        compiler_params=_params("arbitrary", "arbitrary"),
        name="l0_in_prompt",
    )(x, nw, w_in, conv_w)


def _l0_in_sample(x, pre1, pre2, nw, w_in, conv_w, seq):
    m, d = x.shape
    tm = ROW_TILE
    row = pl.BlockSpec((tm, d), lambda i: (i, 0))
    return pl.pallas_call(
        functools.partial(_l0_in_sample_kernel, seq=seq),
        grid=(m // tm,),
        in_specs=[row, row, row,
                  pl.BlockSpec((1, d), lambda i: (0, 0)),
                  pl.BlockSpec((d, 3 * d), lambda i: (0, 0)),
                  pl.BlockSpec((CONV_W, d), lambda i: (0, 0))],
        out_specs=[row, row],
        out_shape=[jax.ShapeDtypeStruct((m, d), BF16), jax.ShapeDtypeStruct((m, d), F32)],
        compiler_params=_params("arbitrary"),
        name="l0_in_sample",
    )(x, pre1, pre2, nw, w_in, conv_w)


def _mix_ffn_kernel(h_ref, z_ref, wo_ref, nw_ref, wg_ref, wu_ref, wd_ref, fnw_ref, o_ref, hn_ref, *, final):
    j = pl.program_id(1)

    @pl.when(j == 0)
    def _():
        h1 = h_ref[...] + _dot(z_ref[...].astype(BF16), wo_ref[...])
        o_ref[...] = h1
        hn_ref[...] = (_rms_unit(h1) * nw_ref[...]).astype(BF16)

    hn = hn_ref[...]
    act = _silu(_dot(hn, wg_ref[...])) * _dot(hn, wu_ref[...])
    o_ref[...] += _dot(act.astype(BF16), wd_ref[...])

    if final:
        @pl.when(j == pl.num_programs(1) - 1)
        def _():
            o_ref[...] = _rms_unit(o_ref[...]) * fnw_ref[...]


def _mix_ffn(h, z, wo, nw, w_in, w_out, layer, fnw, final):
    m, d = h.shape
    kz = z.shape[1]
    f = w_out.shape[1]
    tm, tf = ROW_TILE, f // 2
    nf = f // tf
    return pl.pallas_call(
        functools.partial(_mix_ffn_kernel, final=final),
        grid=(m // tm, nf),
        in_specs=[
            pl.BlockSpec((tm, d), lambda i, j: (i, 0)),
            pl.BlockSpec((tm, kz), lambda i, j: (i, 0)),
            pl.BlockSpec((kz, d), lambda i, j: (0, 0)),
            pl.BlockSpec((1, d), lambda i, j: (0, 0)),
            pl.BlockSpec((None, d, tf), lambda i, j: (layer, 0, j)),
            pl.BlockSpec((None, d, tf), lambda i, j: (layer, 0, j + nf)),
            pl.BlockSpec((None, tf, d), lambda i, j: (layer, j, 0)),
            pl.BlockSpec((1, d), lambda i, j: (0, 0)),
        ],
        out_specs=pl.BlockSpec((tm, d), lambda i, j: (i, 0)),
        out_shape=jax.ShapeDtypeStruct((m, d), F32),
        scratch_shapes=[pltpu.VMEM((tm, d), BF16)],
        compiler_params=_params("arbitrary", "arbitrary"),
        name="mix_ffn_final" if final else "mix_ffn",
    )(h, z, wo, nw, w_in, w_in, w_out, fnw)


def _l1_proj(h_ref, nw_ref, kvnw_ref, wq_ref, wg_ref, wkv_ref, q_ref, g_ref):
    unit = _rms_unit(h_ref[...])
    xn = (unit * nw_ref[...]).astype(BF16)
    kn = (unit * kvnw_ref[...]).astype(BF16)
    q_ref[...] = (_dot(xn, wq_ref[...]) * (HEAD_DIM ** -0.5 * LOG2E)).astype(q_ref.dtype)
    g_ref[...] = jax.nn.sigmoid(_dot(xn, wg_ref[...]))
    return _dot(kn, wkv_ref[...])


def _l1_in_prompt_kernel(h_ref, nw_ref, kvnw_ref, wq_ref, wg_ref, wkv_ref,
                         q_ref, g_ref, cmpc_ref, cmpt_ref, slct_ref, wint_ref, kvtb_ref, stage_ref):
    kv = _l1_proj(h_ref, nw_ref, kvnw_ref, wq_ref, wg_ref, wkv_ref, q_ref, g_ref)
    n_chunk = cmpc_ref.shape[0]
    for p in range(KV_ROW // LANES):
        stage_ref[p] = kv[:, p * LANES:(p + 1) * LANES]
    for j in range(CMP_STRIDE):
        for p in range(KV_ROW // LANES):
            col = j * (KV_ROW // LANES) + p
            cmpc_ref[:, col * LANES:(col + 1) * LANES] = stage_ref[p, pl.ds(j, n_chunk, stride=CMP_STRIDE), :]
    kvt = kv.T
    cmpt_ref[...] = kvt[0:KV_ROW, :]
    slct_ref[...] = kvt[KV_ROW:2 * KV_ROW, :]
    wint_ref[...] = kvt[2 * KV_ROW:3 * KV_ROW, :]
    kvtb_ref[...] = kvt[KV_ROW:3 * KV_ROW, :].astype(BF16)


def _l1_in_sample_kernel(h_ref, nw_ref, kvnw_ref, wq_ref, wg_ref, wkv_ref, q_ref, g_ref, cmp_ref, slc_ref, win_ref):
    kv = _l1_proj(h_ref, nw_ref, kvnw_ref, wq_ref, wg_ref, wkv_ref, q_ref, g_ref)
    cmp_ref[...] = kv[:, 0:KV_ROW]
    slc_ref[...] = kv[:, KV_ROW:2 * KV_ROW]
    win_ref[...] = kv[:, 2 * KV_ROW:3 * KV_ROW]


def _l1_in_prompt(h, nw, kvnw, wq, wg, wkv):
    n, t, d = h.shape
    tm = ROW_TILE
    nq, ng = wq.shape[1], wg.shape[1]
    row = lambda w: pl.BlockSpec((None, tm, w), lambda b, i: (b, i, 0))
    col = lambda r: pl.BlockSpec((None, r, tm), lambda b, i: (b, 0, i))
    full = lambda a: pl.BlockSpec(a.shape, lambda b, i: (0, 0))
    tshape = lambda r, dt: jax.ShapeDtypeStruct((n, r, t), dt)
    return pl.pallas_call(
        _l1_in_prompt_kernel,
        grid=(n, t // tm),
        in_specs=[row(d), full(nw), full(kvnw), full(wq), full(wg), full(wkv)],
        out_specs=[row(nq), row(ng), pl.BlockSpec((None, tm // CMP_STRIDE, CHUNK_LANES), lambda b, i: (b, i, 0)),
                   col(KV_ROW), col(KV_ROW), col(KV_ROW), col(2 * KV_ROW)],
        out_shape=[jax.ShapeDtypeStruct((n, t, nq), BF16), jax.ShapeDtypeStruct((n, t, ng), F32),
                   jax.ShapeDtypeStruct((n, t // CMP_STRIDE, CHUNK_LANES), F32),
                   tshape(KV_ROW, F32), tshape(KV_ROW, F32), tshape(KV_ROW, F32), tshape(2 * KV_ROW, BF16)],
        scratch_shapes=[pltpu.VMEM((KV_ROW // LANES, tm, LANES), F32)],
        compiler_params=_params("arbitrary", "arbitrary"),
        name="l1_in_prompt",
    )(h, nw, kvnw, wq, wg, wkv)


def _l1_in_sample(h, nw, kvnw, wq, wg, wkv):
    m, d = h.shape
    tm = ROW_TILE
    nq, ng = wq.shape[1], wg.shape[1]
    row = lambda w: pl.BlockSpec((tm, w), lambda i: (i, 0))
    full = lambda a: pl.BlockSpec(a.shape, lambda i: (0, 0))
    return pl.pallas_call(
        _l1_in_sample_kernel,
        grid=(m // tm,),
        in_specs=[row(d), full(nw), full(kvnw), full(wq), full(wg), full(wkv)],
        out_specs=[row(nq), row(ng), row(KV_ROW), row(KV_ROW), row(KV_ROW)],
        out_shape=[jax.ShapeDtypeStruct((m, nq), F32), jax.ShapeDtypeStruct((m, ng), F32),
                   jax.ShapeDtypeStruct((m, KV_ROW), F32), jax.ShapeDtypeStruct((m, KV_ROW), F32),
                   jax.ShapeDtypeStruct((m, KV_ROW), F32)],
        compiler_params=_params("arbitrary"),
        name="l1_in_sample",
    )(h, nw, kvnw, wq, wg, wkv)


def _compress_tokens(gather, tc, pe_ref, w1d_ref, w1p_ref, w2_ref, shift_ref):
    pieces = []
    for p in range(KV_ROW // LANES):
        k = p // 2
        xp = jnp.concatenate([gather(p, j) for j in range(CMP_STRIDE)], axis=1).astype(BF16)
        part = _dot(xp, w1p_ref[k])
        shift_ref[...] = part[:, LANES:2 * LANES]
        peb = _dot(pe_ref[k].astype(BF16), w1d_ref[k])
        pre = part[0:tc, 0:LANES] + shift_ref[pl.ds(1, tc), :] + peb
        pieces.append(_dot(_silu(pre).astype(BF16), w2_ref[k]))
    return jnp.concatenate(pieces, axis=1)


def _compress_prompt_kernel(x_ref, halo_ref, pe_ref, w1d_ref, w1p_ref, w2_ref, kc_ref, kct_ref, xs_ref, shift_ref):
    tc = x_ref.shape[0]
    xs_ref[0:tc, :] = x_ref[...]
    xs_ref[tc:tc + SUBLANES, :] = halo_ref[...]
    gather = lambda p, j: xs_ref[:, (j * 4 + p) * LANES:(j * 4 + p + 1) * LANES]
    kc = _compress_tokens(gather, tc, pe_ref, w1d_ref, w1p_ref, w2_ref, shift_ref)
    kc_ref[...] = kc.astype(kc_ref.dtype)
    kct_ref[...] = kc.T.astype(kct_ref.dtype)


def _compress_prompt(chunks, pe, w1d, w1p, w2):
    n, c, _ = chunks.shape
    tc = CMP_TILE
    nhalo = c // SUBLANES
    full = lambda a: pl.BlockSpec(a.shape, lambda b, i: (0,) * a.ndim)
    return pl.pallas_call(
        _compress_prompt_kernel,
        grid=(n, c // tc),
        in_specs=[
            pl.BlockSpec((None, tc, CHUNK_LANES), lambda b, i: (b, i, 0)),
            pl.BlockSpec((None, SUBLANES, CHUNK_LANES),
                         lambda b, i: (b, jnp.minimum((i + 1) * (tc // SUBLANES), nhalo - 1), 0)),
            full(pe), full(w1d), full(w1p), full(w2),
        ],
        out_specs=[pl.BlockSpec((None, tc, KV_ROW), lambda b, i: (b, i, 0)),
                   pl.BlockSpec((None, KV_ROW, tc), lambda b, i: (b, 0, i))],
        out_shape=[jax.ShapeDtypeStruct((n, c, KV_ROW), BF16), jax.ShapeDtypeStruct((n, KV_ROW, c), BF16)],
        scratch_shapes=[pltpu.VMEM((tc + SUBLANES, CHUNK_LANES), F32), pltpu.VMEM((tc + SUBLANES, LANES), F32)],
        compiler_params=_params("arbitrary", "arbitrary"),
        name="compress_prompt",
    )(chunks, chunks, pe, w1d, w1p, w2)


def _compress_sample_kernel(pt_ref, new_ref, pe_ref, w1d_ref, w1p_ref, w2_ref, *rest, n_pages, bps):
    page_refs = rest[:bps * n_pages]
    kc_ref, xs_ref, shift_ref = rest[bps * n_pages:]
    for bi in range(bps):
        _compress_sample_one(new_ref.at[bi], pe_ref, w1d_ref, w1p_ref, w2_ref,
                             page_refs[bi * n_pages:(bi + 1) * n_pages], kc_ref.at[bi], xs_ref.at[bi], shift_ref.at[bi])


def _compress_sample_one(new_ref, pe_ref, w1d_ref, w1p_ref, w2_ref, page_refs, kc_ref, xs_ref, shift_ref):
    n_pages = len(page_refs)
    n_pieces = KV_ROW // LANES
    past = n_pages * PAGE_SIZE
    seq = new_ref.shape[0]
    tc = past // CMP_STRIDE
    per_page = PAGE_SIZE // CMP_STRIDE
    pitch = tc + SUBLANES
    for p in range(n_pieces):
        for j in range(CMP_STRIDE):
            xs_ref[p, j * pitch + tc:(j + 1) * pitch, :] = jnp.zeros((SUBLANES, LANES), F32)
        xs_ref[p, pl.ds(tc, seq, stride=pitch), :] = new_ref[:, p * LANES:(p + 1) * LANES]
    for pg in range(n_pages):
        page = page_refs[pg][...]
        for p in range(n_pieces):
            tile = page[p * LANES:(p + 1) * LANES, :].T
            for v in range(PAGE_SIZE // SUBLANES):
                c, j0 = divmod(v * SUBLANES, CMP_STRIDE)
                xs_ref[p, pl.ds(j0 * pitch + pg * per_page + c, SUBLANES, stride=pitch), :] = (
                    tile[v * SUBLANES:(v + 1) * SUBLANES, :])
    gather = lambda p, j: xs_ref[p, j * pitch:(j + 1) * pitch, :]
    kc = _compress_tokens(gather, tc, pe_ref, w1d_ref, w1p_ref, w2_ref, shift_ref)
    kc_ref[...] = kc.astype(kc_ref.dtype)


def _compress_sample(page_table, cache_t, new_rows, pe, w1d, w1p, w2):
    nb, seq, _ = new_rows.shape
    n_pages = page_table.shape[0] // nb
    bps = SAMPLE_BATCH_PER_STEP
    tc = n_pages * PAGE_SIZE // CMP_STRIDE
    full = lambda a: pl.BlockSpec(a.shape, lambda b, pt: (0,) * a.ndim)
    page_specs = [pl.BlockSpec((None, KV_ROW, PAGE_SIZE),
                               lambda b, pt, bi=bi, p=p: (pt[(b * bps + bi) * n_pages + p], 0, 0))
                  for bi in range(bps) for p in range(n_pages)]
    return pl.pallas_call(
        functools.partial(_compress_sample_kernel, n_pages=n_pages, bps=bps),
        grid_spec=pltpu.PrefetchScalarGridSpec(
            num_scalar_prefetch=1,
            grid=(nb // bps,),
            in_specs=[pl.BlockSpec((bps, seq, KV_ROW), lambda b, pt: (b, 0, 0)),
                      full(pe), full(w1d), full(w1p), full(w2)] + page_specs,
            out_specs=pl.BlockSpec((bps, tc, KV_ROW), lambda b, pt: (b, 0, 0)),
            scratch_shapes=[pltpu.VMEM((bps, KV_ROW // LANES, (tc + SUBLANES) * CMP_STRIDE, LANES), F32),
                            pltpu.VMEM((bps, tc + SUBLANES, LANES), F32)],
        ),
        out_shape=jax.ShapeDtypeStruct((nb, tc, KV_ROW), BF16),
        compiler_params=_params("arbitrary"),
        name="compress_sample",
    )(page_table, new_rows, pe, w1d, w1p, w2, *([cache_t] * (bps * n_pages)))


def _top_n_rows(score, n_pick):
    rows = lax.broadcasted_iota(jnp.int32, score.shape, 0).astype(F32)
    sel = jnp.zeros(score.shape, F32)
    s = score
    for _ in range(n_pick):
        m = jnp.max(s, axis=0, keepdims=True)
        first = jnp.min(jnp.where(s == m, rows, float(score.shape[0])), axis=0, keepdims=True)
        hit = rows == first
        sel = jnp.where(hit, 1.0, sel)
        s = jnp.where(hit, -jnp.inf, s)
    return sel


def _softmax_cols(s, mask):
    s = jnp.where(mask, s, NEG)
    m = jnp.max(s, axis=0, keepdims=True)
    e = jnp.where(mask, jnp.exp2(s - m), 0.0)
    return e * (1.0 / jnp.maximum(jnp.sum(e, axis=0, keepdims=True), 1e-30))


def _softmax_rows_unnorm(s, mask):
    s = jnp.where(mask, s, NEG)
    m = jnp.max(s, axis=1, keepdims=True)
    e = jnp.where(mask, jnp.exp2(s - m), 0.0)
    return e, 1.0 / jnp.maximum(jnp.sum(e, axis=1, keepdims=True), 1e-30)


def _edge_biases(tq):
    t = lax.broadcasted_iota(jnp.int32, (tq, tq), 0)
    i = lax.broadcasted_iota(jnp.int32, (tq, tq), 1)
    return jnp.where(i <= t, 0.0, NEG), jnp.where(i >= t, 0.0, NEG)


def _block_bias(d, causal, lower, lower_at):
    bias = jnp.where(d == 0, causal, jnp.where(d < 0, NEG, 0.0))
    if lower_at is not None:
        bias = jnp.where(d == lower_at, lower, bias)
    return bias


def _add_per_head(s, bias, n_heads):
    tq = bias.shape[0]
    return jnp.concatenate([s[h * tq:(h + 1) * tq] + bias for h in range(n_heads)], axis=0)


def _block_scores(imp, pos, n_sel):
    jj = lax.broadcasted_iota(jnp.int32, imp.shape, 0)
    valid = (jj * SEL_BLK <= pos) & (jj < n_sel)
    cur = pos // SEL_BLK
    forced = (jj == 0) | (jj == cur) | (jj == cur - 1)
    score = jnp.where(valid & forced, BIG, jnp.where(valid, imp, -BIG))
    return jnp.where(jj < n_sel, score, -jnp.inf)


def _split_dot(w_bf16, x):
    hi = x.astype(BF16)
    lo = (x - hi.astype(F32)).astype(BF16)
    return _dot(w_bf16, hi) + _dot(w_bf16, lo)


def _with_ones(v_half, ones, first):
    return jnp.concatenate([v_half, ones] if first else [ones, v_half], axis=0)


def _norm_by_other_half(acc):
    return acc * (1.0 / jnp.maximum(pltpu.roll(acc, HEAD_DIM, axis=1), 1e-30))


def _attn_prompt_kernel(q_ref, g_ref, kc_ref, vct_ref, skt_ref, svt_ref, wkt_ref, wvt_ref, et_ref, aggt_ref, z_ref):
    tq = q_ref.shape[0]
    tkc = KEY_CHUNK
    qt = pl.program_id(2)
    t0 = qt * tq
    heads = 2 * GROUP
    r_rows = heads * tq
    n_cmp = kc_ref.shape[0]
    n_sel = et_ref.shape[1] // SEL_BLK
    half = lax.broadcasted_iota(jnp.int32, (tq, LANES), 1) // HEAD_DIM
    qpos_row = t0 + (lax.broadcasted_iota(jnp.int32, (1, r_rows), 1) & (tq - 1))

    qs = [q_ref[:, h * LANES:(h + 1) * LANES] for h in range(heads)]
    qall = jnp.concatenate(qs, axis=0)

    causal_b, lower_b = _edge_biases(tq)

    def pv_ones(p, v, ones_rows):
        return jnp.concatenate(
            [_dot_nt(p[gg * GROUP * tq:(gg + 1) * GROUP * tq],
                     _with_ones(v[gg * HEAD_DIM:(gg + 1) * HEAD_DIM], ones_rows, gg == 0)) for gg in range(2)], axis=0)

    cend = lax.broadcasted_iota(jnp.int32, (n_cmp, 1), 0) * CMP_STRIDE + (CMP_BLK - 1)
    pt = _softmax_cols(_dot_nt(kc_ref[...], qall), cend <= qpos_row)
    psum = []
    for gg in range(2):
        acc = pt[:, gg * GROUP * tq:(gg * GROUP + 1) * tq]
        for r in range(1, GROUP):
            acc = acc + pt[:, (gg * GROUP + r) * tq:(gg * GROUP + r + 1) * tq]
        psum.append(acc)
    imp = _split_dot(aggt_ref[...], jnp.concatenate(psum, axis=1))
    oct = _dot(vct_ref[...], pt.astype(BF16))

    n_wblk = WINDOW // tq + 1
    start = pl.multiple_of(jnp.maximum(t0 - WINDOW, 0), LANES)
    back0 = jnp.minimum(WINDOW // tq, qt)
    sw = jnp.concatenate(
        [_add_per_head(_dot(qall, wkt_ref[:, pl.ds(pl.multiple_of(start + b * tq, LANES), tq)]),
                       _block_bias(back0 - b, causal_b, lower_b, WINDOW // tq), heads) for b in range(n_wblk)],
        axis=1).astype(BF16)
    ew = jnp.exp2(sw - jnp.max(sw, axis=1, keepdims=True))
    o_w = _norm_by_other_half(pv_ones(ew, wvt_ref[:, pl.ds(start, n_wblk * tq)], jnp.ones((HEAD_DIM, n_wblk * tq), BF16)))

    gates = g_ref[...]
    part = [gates[:, 3 * h:3 * h + 1] * oct[:, h * tq:(h + 1) * tq].T
            + gates[:, 3 * h + 2:3 * h + 3] * o_w[h * tq:(h + 1) * tq, :] for h in range(heads)]

    pos = t0 + (lax.broadcasted_iota(jnp.int32, imp.shape, 1) & (tq - 1))
    selt = _top_n_rows(_block_scores(imp, pos, n_sel), min(TOP_N, n_sel))
    selm1 = [(selt[:, gg * tq:(gg + 1) * tq].T - 1.0).astype(BF16) for gg in range(2)]
    qaug = jnp.concatenate([jnp.concatenate([qs[h], selm1[h // GROUP]], axis=1) for h in range(heads)], axis=0)

    def sel_chunk(start, width, carry, diag_block):
        m_i, acc = carry
        keys = pl.ds(start, width)
        rhs = jnp.concatenate([skt_ref[:, keys], et_ref[:, keys]], axis=0)
        s = _dot(qaug, rhs)
        if diag_block is not None:
            s = jnp.concatenate(
                [_add_per_head(s[:, b * tq:(b + 1) * tq], _block_bias(diag_block - b, causal_b, lower_b, None), heads)
                 for b in range(width // tq)], axis=1)
        sb = s.astype(BF16)
        m_new = jnp.maximum(m_i, jnp.max(sb, axis=1, keepdims=True).astype(F32))
        alpha = jnp.exp2(m_i - m_new)
        p = jnp.exp2(sb - m_new.astype(BF16))
        return m_new, alpha * acc + pv_ones(p, svt_ref[:, keys], jnp.ones((HEAD_DIM, width), BF16))

    n_full = t0 // tkc
    n_wide = n_full // 2
    aligned = lambda c, width: pl.multiple_of(c * width, width)
    init = (jnp.full((r_rows, 1), M_INIT, F32), jnp.zeros((r_rows, LANES), F32))
    carry = sel_chunk(aligned(n_full, tkc), tkc, init, qt - n_full * (tkc // tq))
    carry = lax.fori_loop(0, n_wide, lambda c, cr: sel_chunk(aligned(c, 2 * tkc), 2 * tkc, cr, None), carry)
    _, acc_s = lax.fori_loop(2 * n_wide, n_full, lambda c, cr: sel_chunk(aligned(c, tkc), tkc, cr, None), carry)
    o_s = _norm_by_other_half(acc_s)

    head_out = [part[h] + gates[:, 3 * h + 1:3 * h + 2] * o_s[h * tq:(h + 1) * tq, :] for h in range(heads)]
    for i in range(heads // 2):
        gg = (2 * i) // GROUP
        a, b = head_out[2 * i], head_out[2 * i + 1]
        packed = (jnp.where(half == 0, a, pltpu.roll(b, HEAD_DIM, axis=1)) if gg == 0
                  else jnp.where(half == 0, pltpu.roll(a, HEAD_DIM, axis=1), b))
        z_ref[:, i * LANES:(i + 1) * LANES] = packed.astype(z_ref.dtype)


def _attn_prompt(q2, gates2, kvc, kvct, kvtb, e_t, aggt):
    n, t, _ = q2.shape
    tq = Q_TILE
    npair = N_KV_HEADS // 2
    pair_w = 2 * GROUP * LANES
    kv_slab = lambda branch, kv: pl.BlockSpec(
        (None, LANES, t), lambda b, gp, i: (b, branch * 2 * npair + kv * npair + gp, 0))
    full = lambda a: pl.BlockSpec(a.shape, lambda b, gp, i: (0,) * a.ndim)
    return pl.pallas_call(
        _attn_prompt_kernel,
        grid=(n, npair, t // tq),
        in_specs=[
            pl.BlockSpec((None, tq, pair_w), lambda b, gp, i: (b, i, gp)),
            pl.BlockSpec((None, tq, LANES), lambda b, gp, i: (b, i, gp)),
            pl.BlockSpec((None, kvc.shape[1], LANES), lambda b, gp, i: (b, 0, gp)),
            pl.BlockSpec((None, LANES, kvct.shape[2]), lambda b, gp, i: (b, npair + gp, 0)),
            kv_slab(0, 0), kv_slab(0, 1), kv_slab(1, 0), kv_slab(1, 1),
            full(e_t), full(aggt),
        ],
        out_specs=pl.BlockSpec((None, tq, 2 * GROUP * HEAD_DIM), lambda b, gp, i: (b, i, gp)),
        out_shape=jax.ShapeDtypeStruct((n, t, N_HEADS * HEAD_DIM), BF16),
        compiler_params=_params("arbitrary", "arbitrary", "arbitrary"),
        name="attn_prompt",
    )(q2, gates2, kvc, kvct, kvtb, kvtb, kvtb, kvtb, e_t, aggt)


def _attn_sample_kernel(pt_ref, qbd_ref, g_ref, kvc_ref, nslc_ref, wst_ref, nwin_ref, et_ref, enew_ref, aggt_ref,
                        *rest, n_pages, past_len, bps):
    page_refs = rest[:bps * n_pages]
    z_ref, wout_ref = rest[bps * n_pages:]
    seq = qbd_ref.shape[1]
    rows = GROUP * N_KV_HEADS * seq
    n_sel = -(-(n_pages * PAGE_SIZE + seq) // SEL_BLK)
    front = [_attn_sample_front(qbd_ref.at[bi], kvc_ref.at[bi], aggt_ref, past_len, n_sel) for bi in range(bps)]
    selt = _top_n_rows(jnp.concatenate([f[2] for f in front], axis=1), min(TOP_N, n_sel))
    for bi in range(bps):
        _attn_sample_back(front[bi][0], front[bi][1], selt[:, bi * rows:(bi + 1) * rows], g_ref.at[bi],
                          nslc_ref.at[bi], wst_ref.at[bi], nwin_ref.at[bi], et_ref, enew_ref,
                          page_refs[bi * n_pages:(bi + 1) * n_pages], z_ref.at[bi], wout_ref.at[bi], past_len)


def _attn_sample_front(qbd_ref, kvc_ref, aggt_ref, past_len, n_sel):
    seq = qbd_ref.shape[0]
    kw = KV_HALF
    rows = GROUP * N_KV_HEADS * seq
    n_cmp = kvc_ref.shape[0]

    qbd = jnp.concatenate(
        [qbd_ref[:, (g * GROUP + r) * kw:(g * GROUP + r + 1) * kw] for r in range(GROUP) for g in range(N_KV_HEADS)],
        axis=0).astype(BF16)
    qpos_row = past_len + (lax.broadcasted_iota(jnp.int32, (1, rows), 1) & (seq - 1))

    cend = lax.broadcasted_iota(jnp.int32, (n_cmp, 1), 0) * CMP_STRIDE + (CMP_BLK - 1)
    pt = _softmax_cols(_dot_nt(kvc_ref[:, 0:kw], qbd), cend <= qpos_row)
    imp = _split_dot(aggt_ref[...], pt)
    per_r = N_KV_HEADS * seq
    imp_g = imp
    for r in range(1, GROUP):
        imp_g = imp_g + pltpu.roll(imp, r * per_r, axis=1)
    vct = kvc_ref[:, kw:2 * kw].astype(F32).T.astype(BF16)
    o_c = _dot(vct, pt.astype(BF16)).T
    return qbd, o_c, _block_scores(imp_g, jnp.broadcast_to(qpos_row, imp_g.shape), n_sel)


def _attn_sample_back(qbd, o_c, selt, g_ref, nslc_ref, wst_ref, nwin_ref, et_ref, enew_ref, page_refs,
                      z_ref, wout_ref, past_len):
    n_pages = len(page_refs)
    seq = g_ref.shape[0]
    kw = KV_HALF
    rows = GROUP * N_KV_HEADS * seq
    w_len = wst_ref.shape[1]
    qpos_col = past_len + (lax.broadcasted_iota(jnp.int32, (rows, 1), 0) & (seq - 1))
    selm1 = (selt.T - 1.0).astype(BF16)
    qaug = jnp.concatenate([qbd, selm1], axis=1)

    pad = jnp.zeros((PAGE_SIZE - seq, kw), F32)
    new_tile = lambda ref, lo: jnp.concatenate([ref[:, lo:lo + kw], pad], axis=0).astype(BF16)

    s_tiles = []
    for p in range(n_pages):
        rhs = jnp.concatenate([page_refs[p][0:kw, :].astype(BF16), et_ref[:, p * PAGE_SIZE:(p + 1) * PAGE_SIZE]], axis=0)
        s_tiles.append(_dot(qaug, rhs))
    s_tiles.append(_dot_nt(qaug, jnp.concatenate([new_tile(nslc_ref, 0), enew_ref[...]], axis=1)))
    s_all = jnp.concatenate(s_tiles, axis=1)
    kpos = lax.broadcasted_iota(jnp.int32, (1, s_all.shape[1]), 1)
    es, inv_s = _softmax_rows_unnorm(s_all, kpos <= qpos_col)
    es = es.astype(BF16)
    o_s = _dot(es[:, n_pages * PAGE_SIZE:], new_tile(nslc_ref, kw))
    for p in range(n_pages):
        o_s = o_s + _dot_nt(es[:, p * PAGE_SIZE:(p + 1) * PAGE_SIZE], page_refs[p][kw:2 * kw, :].astype(BF16))
    o_s = o_s * inv_s

    wst = wst_ref[...]
    sw = jnp.concatenate([_dot(qbd, wst[0:kw, :].astype(BF16)), _dot_nt(qbd, new_tile(nwin_ref, 0))], axis=1)
    kpos_w = past_len - w_len + lax.broadcasted_iota(jnp.int32, (1, w_len + PAGE_SIZE), 1)
    mw = (kpos_w <= qpos_col) & (kpos_w >= qpos_col - WINDOW) & (kpos_w >= 0)
    ew, inv_w = _softmax_rows_unnorm(sw, mw)
    ew = ew.astype(BF16)
    o_w = (_dot_nt(ew[:, 0:w_len], wst[kw:2 * kw, :].astype(BF16)) + _dot(ew[:, w_len:], new_tile(nwin_ref, kw))) * inv_w

    gates = g_ref[...]
    half = lax.broadcasted_iota(jnp.int32, (seq, LANES), 1) // HEAD_DIM

    def head_out(g, r):
        rs = slice((r * N_KV_HEADS + g) * seq, (r * N_KV_HEADS + g + 1) * seq)
        cs = slice((g // 2) * LANES, (g // 2 + 1) * LANES)
        c = (g * GROUP + r) * 3
        return (gates[:, c:c + 1] * o_c[rs, cs] + gates[:, c + 1:c + 2] * o_s[rs, cs]
                + gates[:, c + 2:c + 3] * o_w[rs, cs])

    for g in range(N_KV_HEADS):
        for i in range(GROUP // 2):
            a, b = head_out(g, 2 * i), head_out(g, 2 * i + 1)
            packed = (jnp.where(half == 0, a, pltpu.roll(b, HEAD_DIM, axis=1)) if g % 2 == 0
                      else jnp.where(half == 0, pltpu.roll(a, HEAD_DIM, axis=1), b))
            col = g * (GROUP // 2) + i
            z_ref[:, col * LANES:(col + 1) * LANES] = packed

    new_t = jnp.concatenate([nwin_ref[...], jnp.zeros((LANES - seq, KV_ROW), F32)], axis=0).T
    shifted = pltpu.roll(wst, w_len - seq, axis=1)
    lane = lax.broadcasted_iota(jnp.int32, (KV_ROW, LANES), 1)
    wout_ref[:, 0:w_len - LANES] = shifted[:, 0:w_len - LANES]
    wout_ref[:, w_len - LANES:w_len] = jnp.where(lane >= LANES - seq, pltpu.roll(new_t, LANES - seq, axis=1),
                                                 shifted[:, w_len - LANES:w_len])


def _attn_sample(page_table, qbd, gates, kvc, new_slc, cache_t, win_t, new_win, e_t, e_new, aggt, past_len):
    nb, seq, _ = qbd.shape
    n_pages = page_table.shape[0] // nb
    w_len = win_t.shape[2]
    assert w_len == min(WINDOW, past_len + seq) and w_len % LANES == 0
    bps = SAMPLE_BATCH_PER_STEP
    full = lambda a: pl.BlockSpec(a.shape, lambda b, pt: (0,) * a.ndim)
    per_b = lambda a: pl.BlockSpec((bps,) + a.shape[1:], lambda b, pt: (b,) + (0,) * (a.ndim - 1))
    page_specs = [pl.BlockSpec((None, KV_ROW, PAGE_SIZE),
                               lambda b, pt, bi=bi, p=p: (pt[(b * bps + bi) * n_pages + p], 0, 0))
                  for bi in range(bps) for p in range(n_pages)]
    zw = N_HEADS * HEAD_DIM
    return pl.pallas_call(
        functools.partial(_attn_sample_kernel, n_pages=n_pages, past_len=past_len, bps=bps),
        grid_spec=pltpu.PrefetchScalarGridSpec(
            num_scalar_prefetch=1,
            grid=(nb // bps,),
            in_specs=[per_b(qbd), per_b(gates), per_b(kvc), per_b(new_slc), per_b(win_t), per_b(new_win),
                      full(e_t), full(e_new), full(aggt)] + page_specs,
            out_specs=[pl.BlockSpec((bps, seq, zw), lambda b, pt: (b, 0, 0)),
                       pl.BlockSpec((bps, KV_ROW, w_len), lambda b, pt: (b, 0, 0))],
        ),
        out_shape=[jax.ShapeDtypeStruct((nb, seq, zw), F32), jax.ShapeDtypeStruct((nb, KV_ROW, w_len), F32)],
        compiler_params=_params("arbitrary"),
        name="attn_sample",
    )(page_table, qbd, gates, kvc, new_slc, win_t, new_win, e_t, e_new, aggt, *([cache_t] * (bps * n_pages)))


def _pad_cols(w, width):
    return jnp.pad(w, ((0, 0), (0, width - w.shape[1])))


def _pack_weights(b_in_w, cmp_w1, cmp_w2):
    d = b_in_w.shape[0]
    nq = N_HEADS * HEAD_DIM
    wq = b_in_w[:, :nq].reshape(d, N_KV_HEADS, GROUP, HEAD_DIM)
    wgate = b_in_w[:, nq:]
    eye2 = jnp.eye(2, dtype=F32)
    eye4 = jnp.eye(N_KV_HEADS, dtype=F32)
    par = jnp.arange(N_KV_HEADS) % 2
    sel2 = jax.nn.one_hot(par, 2, dtype=F32)
    wq_p = jnp.einsum('dgrh,ga->dgrah', wq, sel2).reshape(d, N_HEADS * LANES)
    wq_s = jnp.einsum('dgrh,ga->dgrah', wq, eye4).reshape(d, N_HEADS * N_KV_HEADS * HEAD_DIM)
    per_pair = 2 * GROUP * 3
    wg_p = jnp.concatenate([_pad_cols(wgate[:, i * per_pair:(i + 1) * per_pair], LANES)
                            for i in range(N_KV_HEADS // 2)], axis=1)
    wg_s = _pad_cols(wgate, LANES)
    w1 = cmp_w1.reshape(2, CMP_BLK // CMP_STRIDE, CMP_STRIDE, HEAD_DIM, -1)
    hid = w1.shape[-1]
    w1p = jnp.einsum('krjdh,ab->kjadrbh', w1, eye2).reshape(2, CMP_STRIDE * LANES, 2 * 2 * hid)
    w1d = jnp.concatenate([cmp_w1.reshape(2, CMP_BLK * HEAD_DIM, hid)] * 2, axis=2)
    w2p = jnp.einsum('khe,ab->kahbe', cmp_w2, eye2).reshape(2, 2 * hid, 2 * HEAD_DIM)
    cast = lambda a: a.astype(BF16)
    return tuple(map(cast, (wq_p, wq_s, wg_p, wg_s, w1p, w1d, w2p)))


def _agg_t(n_cmp, n_sel_pad):
    c0 = jnp.arange(n_cmp) * CMP_STRIDE
    s0 = jnp.arange(n_sel_pad) * SEL_BLK
    ov = jnp.clip(jnp.minimum(c0[None, :] + CMP_BLK, s0[:, None] + SEL_BLK) - jnp.maximum(c0[None, :], s0[:, None]),
                  0, None)
    return (ov.astype(F32) / CMP_STRIDE).astype(BF16)


def _block_onehot(key0, n_keys):
    blk = (key0 + jnp.arange(n_keys)) // SEL_BLK
    return (jax.nn.one_hot(blk, LANES, dtype=F32) * BIG).astype(BF16)


def _feature_major(a):
    return a.transpose(0, 2, 3, 4, 1).reshape(a.shape[0], KV_ROW, a.shape[1])


def _position_major(a_t):
    b, _, pos = a_t.shape
    return a_t.reshape(b, 2, N_KV_HEADS, HEAD_DIM, pos).transpose(0, 4, 1, 2, 3)


def kernel(x_prompt, x_sample, cache_cmp_kv, cache_slc_kv, state_win_kv, state_conv, page_table, norm_w, final_norm_w, a_in_w, a_conv_w, a_out_w, b_in_w, b_out_w, kv_norm_w, kv_w, cmp_pe, cmp_w1, cmp_w2, ffn_in_w, ffn_out_w):
    n, t, d = x_prompt.shape
    nb, ts, _ = x_sample.shape
    n_pages = page_table.shape[1]
    past_len = n_pages * PAGE_SIZE
    d_ff = ffn_out_w.shape[1]
    assert a_in_w.shape[0] == 1 and b_in_w.shape[0] == 1 and d == N_HEADS * HEAD_DIM
    assert t % (CMP_TILE * CMP_STRIDE) == 0 and t % KEY_CHUNK == 0 and KEY_CHUNK % Q_TILE == 0
    assert ts == SUBLANES and (nb * ts) % ROW_TILE == 0 and d_ff % (2 * LANES) == 0
    assert nb % SAMPLE_BATCH_PER_STEP == 0 and WINDOW % Q_TILE == 0

    wq_p, wq_s, wg_p, wg_s, w1p, w1d, w2p = _pack_weights(b_in_w[0], cmp_w1, cmp_w2)
    a_in, a_out, kvw = a_in_w[0].astype(BF16), a_out_w[0].astype(BF16), kv_w.astype(BF16)
    b_out = b_out_w[0].astype(BF16)
    ffn_in, ffn_out = ffn_in_w.astype(BF16), ffn_out_w.astype(BF16)
    nw = norm_w.reshape(norm_w.shape[0], 2, 1, d)
    fnw = final_norm_w.reshape(1, d)
    kvnw = kv_norm_w.reshape(1, d)
    pe = cmp_pe.reshape(2, 1, CMP_BLK * HEAD_DIM)
    mp, ms = n * t, nb * ts

    z0, tail_p = _l0_in_prompt(x_prompt, nw[0, 0], a_in, a_conv_w[0])
    h1 = _mix_ffn(x_prompt.reshape(mp, d), z0.reshape(mp, d), a_out, nw[0, 1], ffn_in, ffn_out, 0, fnw, final=False)
    q2, g2, cmp_chunks, cmp_t, slc_t, win_t, kvtb = _l1_in_prompt(h1.reshape(n, t, d), nw[1, 0], kvnw, wq_p, wg_p, kvw)
    n_chunks = t // CMP_STRIDE
    kvc, kvct = _compress_prompt(cmp_chunks, pe, w1d, w1p, w2p)
    z1 = _attn_prompt(q2, g2, kvc, kvct, kvtb, _block_onehot(0, t).T, _agg_t(n_chunks, LANES))
    y_p = _mix_ffn(h1, z1.reshape(mp, -1), b_out, nw[1, 1], ffn_in, ffn_out, 1, fnw, final=True)

    zero = jnp.zeros((nb, ts - 2, d), F32)
    pre1 = jnp.concatenate([state_conv[0, :, 1:2], jnp.zeros((nb, ts - 1, d), F32)], axis=1).reshape(ms, d)
    pre2 = jnp.concatenate([state_conv[0], zero], axis=1).reshape(ms, d)
    xs = x_sample.reshape(ms, d)
    z0s, v_s = _l0_in_sample(xs, pre1, pre2, nw[0, 0], a_in, a_conv_w[0], seq=ts)
    h1s = _mix_ffn(xs, z0s, a_out, nw[0, 1], ffn_in, ffn_out, 0, fnw, final=False)
    qbd, gs, cmp_s, slc_s, win_s_new = _l1_in_sample(h1s, nw[1, 0], kvnw, wq_s, wg_s, kvw)
    pt_flat = page_table.reshape(-1)
    kvc_s = _compress_sample(pt_flat, _feature_major(cache_cmp_kv), cmp_s.reshape(nb, ts, KV_ROW), pe, w1d, w1p, w2p)
    z1s, win_s_t = _attn_sample(pt_flat, qbd.reshape(nb, ts, -1), gs.reshape(nb, ts, -1), kvc_s,
                                slc_s.reshape(nb, ts, KV_ROW), _feature_major(cache_slc_kv),
                                _feature_major(state_win_kv), win_s_new.reshape(nb, ts, KV_ROW),
                                _block_onehot(0, past_len).T, _block_onehot(past_len, PAGE_SIZE),
                                _agg_t(kvc_s.shape[1], LANES), past_len)
    y_s = _mix_ffn(h1s, z1s.reshape(ms, -1), b_out, nw[1, 1], ffn_in, ffn_out, 1, fnw, final=True)

    kv_shape = (2, N_KV_HEADS, HEAD_DIM)
    keep_p = min(WINDOW, t)
    return (
        y_p.reshape(n, t, d),
        y_s.reshape(nb, ts, d),
        tail_p[:, SUBLANES - (CONV_W - 1):][None],
        v_s.reshape(nb, ts, d)[:, ts - (CONV_W - 1):][None],
        _position_major(cmp_t),
        cmp_s.reshape((nb, ts) + kv_shape),
        _position_major(slc_t),
        slc_s.reshape((nb, ts) + kv_shape),
        _position_major(win_t[:, :, t - keep_p:]),
        _position_major(win_s_t),
    )
```

```python
import functools

import jax
import jax.numpy as jnp
from jax import lax
from jax.experimental import pallas as pl
from jax.experimental.pallas import tpu as pltpu

F32 = jnp.float32
BF16 = jnp.bfloat16

HEAD_DIM = 64
N_HEADS = 16
N_KV_HEADS = 4
GROUP = N_HEADS // N_KV_HEADS
CMP_BLK = 32
CMP_STRIDE = 16
SEL_BLK = 64
TOP_N = 16
WINDOW = 512
PAGE_SIZE = 128
CONV_W = 3
RMS_EPS = 1e-6
BIG = 1e9
NEG = -1e30
LOG2E = 1.4426950408889634
M_INIT = -(2.0 ** 100)

LANES = 128
SUBLANES = 8
KV_ROW = 2 * N_KV_HEADS * HEAD_DIM
KV_HALF = KV_ROW // 2
CHUNK_LANES = CMP_STRIDE * KV_ROW
VMEM_LIMIT_BYTES = 56 * 1024 * 1024

ROW_TILE = 512
CMP_TILE = 128
Q_TILE = 256
KEY_CHUNK = 512
SAMPLE_BATCH_PER_STEP = 4

_NT = (((1,), (1,)), ((), ()))


def _dot(a, b):
    return jnp.dot(a, b, preferred_element_type=F32)


def _dot_nt(a, b):
    return lax.dot_general(a, b, _NT, preferred_element_type=F32)


def _rms_unit(x):
    return x * lax.rsqrt(jnp.mean(x * x, axis=-1, keepdims=True) + RMS_EPS)


def _silu(x):
    return x * jax.nn.sigmoid(x)


def _params(*sem):
    return pltpu.CompilerParams(dimension_semantics=sem, vmem_limit_bytes=VMEM_LIMIT_BYTES)


def _conv_gate(xn_bf16, win_ref, cw_ref, vm1_fix, vm2_fix, d):
    b = _dot(xn_bf16, win_ref[:, 0:d])
    c = _dot(xn_bf16, win_ref[:, d:2 * d])
    u = _dot(xn_bf16, win_ref[:, 2 * d:3 * d])
    v = c * u
    vm1 = vm1_fix(pltpu.roll(v, 1, axis=0))
    vm2 = vm2_fix(pltpu.roll(v, 2, axis=0))
    cw = cw_ref[...]
    conv = cw[0:1, :] * vm2 + cw[1:2, :] * vm1 + cw[2:3, :] * v
    return b * conv, v


def _l0_in_prompt_kernel(x_ref, nw_ref, win_ref, cw_ref, z_ref, tail_ref, carry_ref):
    i = pl.program_id(1)
    tm, d = x_ref.shape

    @pl.when(i == 0)
    def _():
        carry_ref[...] = jnp.zeros_like(carry_ref)

    xn = (_rms_unit(x_ref[...]) * nw_ref[...]).astype(BF16)
    prev = carry_ref[...]
    row = lax.broadcasted_iota(jnp.int32, (tm, 1), 0)
    fix1 = lambda r: jnp.where(row == 0, prev[7:8, :], r)
    fix2 = lambda r: jnp.where(row == 0, prev[6:7, :], jnp.where(row == 1, prev[7:8, :], r))
    z, v = _conv_gate(xn, win_ref, cw_ref, fix1, fix2, d)
    z_ref[...] = z.astype(z_ref.dtype)
    carry_ref[...] = v[tm - SUBLANES:tm, :]

    @pl.when(i == pl.num_programs(1) - 1)
    def _():
        tail_ref[...] = v[tm - SUBLANES:tm, :]


def _l0_in_sample_kernel(x_ref, p1_ref, p2_ref, nw_ref, win_ref, cw_ref, z_ref, v_ref, *, seq):
    tm, d = x_ref.shape
    xn = (_rms_unit(x_ref[...]) * nw_ref[...]).astype(BF16)
    t = lax.broadcasted_iota(jnp.int32, (tm, 1), 0) & (seq - 1)
    fix1 = lambda r: jnp.where(t == 0, p1_ref[...], r)
    fix2 = lambda r: jnp.where(t < 2, p2_ref[...], r)
    z, v = _conv_gate(xn, win_ref, cw_ref, fix1, fix2, d)
    z_ref[...] = z.astype(z_ref.dtype)
    v_ref[...] = v


def _l0_in_prompt(x, nw, w_in, conv_w):
    n, t, d = x.shape
    tm = ROW_TILE
    return pl.pallas_call(
        _l0_in_prompt_kernel,
        grid=(n, t // tm),
        in_specs=[
            pl.BlockSpec((None, tm, d), lambda b, i: (b, i, 0)),
            pl.BlockSpec((1, d), lambda b, i: (0, 0)),
            pl.BlockSpec((d, 3 * d), lambda b, i: (0, 0)),
            pl.BlockSpec((CONV_W, d), lambda b, i: (0, 0)),
        ],
        out_specs=[
            pl.BlockSpec((None, tm, d), lambda b, i: (b, i, 0)),
            pl.BlockSpec((None, SUBLANES, d), lambda b, i: (b, 0, 0)),
        ],
        out_shape=[jax.ShapeDtypeStruct((n, t, d), BF16), jax.ShapeDtypeStruct((n, SUBLANES, d), F32)],
        scratch_shapes=[pltpu.VMEM((SUBLANES, d), F32)],
        compiler_params=_params("arbitrary", "arbitrary"),
        name="l0_in_prompt",
    )(x, nw, w_in, conv_w)


def _l0_in_sample(x, pre1, pre2, nw, w_in, conv_w, seq):
    m, d = x.shape
    tm = ROW_TILE
    row = pl.BlockSpec((tm, d), lambda i: (i, 0))
    return pl.pallas_call(
        functools.partial(_l0_in_sample_kernel, seq=seq),
        grid=(m // tm,),
        in_specs=[row, row, row,
                  pl.BlockSpec((1, d), lambda i: (0, 0)),
                  pl.BlockSpec((d, 3 * d), lambda i: (0, 0)),
                  pl.BlockSpec((CONV_W, d), lambda i: (0, 0))],
        out_specs=[row, row],
        out_shape=[jax.ShapeDtypeStruct((m, d), BF16), jax.ShapeDtypeStruct((m, d), F32)],
        compiler_params=_params("arbitrary"),
        name="l0_in_sample",
    )(x, pre1, pre2, nw, w_in, conv_w)


def _mix_ffn_kernel(h_ref, z_ref, wo_ref, nw_ref, wg_ref, wu_ref, wd_ref, fnw_ref, o_ref, hn_ref, *, final):
    j = pl.program_id(1)

    @pl.when(j == 0)
    def _():
        h1 = h_ref[...] + _dot(z_ref[...].astype(BF16), wo_ref[...])
        o_ref[...] = h1
        hn_ref[...] = (_rms_unit(h1) * nw_ref[...]).astype(BF16)

    hn = hn_ref[...]
    act = _silu(_dot(hn, wg_ref[...])) * _dot(hn, wu_ref[...])
    o_ref[...] += _dot(act.astype(BF16), wd_ref[...])

    if final:
        @pl.when(j == pl.num_programs(1) - 1)
        def _():
            o_ref[...] = _rms_unit(o_ref[...]) * fnw_ref[...]


def _mix_ffn(h, z, wo, nw, w_in, w_out, layer, fnw, final):
    m, d = h.shape
    kz = z.shape[1]
    f = w_out.shape[1]
    tm, tf = ROW_TILE, f // 2
    nf = f // tf
    return pl.pallas_call(
        functools.partial(_mix_ffn_kernel, final=final),
        grid=(m // tm, nf),
        in_specs=[
            pl.BlockSpec((tm, d), lambda i, j: (i, 0)),
            pl.BlockSpec((tm, kz), lambda i, j: (i, 0)),
            pl.BlockSpec((kz, d), lambda i, j: (0, 0)),
            pl.BlockSpec((1, d), lambda i, j: (0, 0)),
            pl.BlockSpec((None, d, tf), lambda i, j: (layer, 0, j)),
            pl.BlockSpec((None, d, tf), lambda i, j: (layer, 0, j + nf)),
            pl.BlockSpec((None, tf, d), lambda i, j: (layer, j, 0)),
            pl.BlockSpec((1, d), lambda i, j: (0, 0)),
        ],
        out_specs=pl.BlockSpec((tm, d), lambda i, j: (i, 0)),
        out_shape=jax.ShapeDtypeStruct((m, d), F32),
        scratch_shapes=[pltpu.VMEM((tm, d), BF16)],
        compiler_params=_params("arbitrary", "arbitrary"),
        name="mix_ffn_final" if final else "mix_ffn",
    )(h, z, wo, nw, w_in, w_in, w_out, fnw)


def _l1_proj(h_ref, nw_ref, kvnw_ref, wq_ref, wg_ref, wkv_ref, q_ref, g_ref):
    unit = _rms_unit(h_ref[...])
    xn = (unit * nw_ref[...]).astype(BF16)
    kn = (unit * kvnw_ref[...]).astype(BF16)
    q_ref[...] = (_dot(xn, wq_ref[...]) * (HEAD_DIM ** -0.5 * LOG2E)).astype(q_ref.dtype)
    g_ref[...] = jax.nn.sigmoid(_dot(xn, wg_ref[...]))
    return _dot(kn, wkv_ref[...])


def _l1_in_prompt_kernel(h_ref, nw_ref, kvnw_ref, wq_ref, wg_ref, wkv_ref,
                         q_ref, g_ref, cmpc_ref, cmpt_ref, slct_ref, wint_ref, kvtb_ref, stage_ref):
    kv = _l1_proj(h_ref, nw_ref, kvnw_ref, wq_ref, wg_ref, wkv_ref, q_ref, g_ref)
    n_chunk = cmpc_ref.shape[0]
    for p in range(KV_ROW // LANES):
        stage_ref[p] = kv[:, p * LANES:(p + 1) * LANES]
    for j in range(CMP_STRIDE):
        for p in range(KV_ROW // LANES):
            col = j * (KV_ROW // LANES) + p
            cmpc_ref[:, col * LANES:(col + 1) * LANES] = stage_ref[p, pl.ds(j, n_chunk, stride=CMP_STRIDE), :]
    kvt = kv.T
    cmpt_ref[...] = kvt[0:KV_ROW, :]
    slct_ref[...] = kvt[KV_ROW:2 * KV_ROW, :]
    wint_ref[...] = kvt[2 * KV_ROW:3 * KV_ROW, :]
    kvtb_ref[...] = kvt[KV_ROW:3 * KV_ROW, :].astype(BF16)


def _l1_in_sample_kernel(h_ref, nw_ref, kvnw_ref, wq_ref, wg_ref, wkv_ref, q_ref, g_ref, cmp_ref, slc_ref, win_ref):
    kv = _l1_proj(h_ref, nw_ref, kvnw_ref, wq_ref, wg_ref, wkv_ref, q_ref, g_ref)
    cmp_ref[...] = kv[:, 0:KV_ROW]
    slc_ref[...] = kv[:, KV_ROW:2 * KV_ROW]
    win_ref[...] = kv[:, 2 * KV_ROW:3 * KV_ROW]


def _l1_in_prompt(h, nw, kvnw, wq, wg, wkv):
    n, t, d = h.shape
    tm = ROW_TILE
    nq, ng = wq.shape[1], wg.shape[1]
    row = lambda w: pl.BlockSpec((None, tm, w), lambda b, i: (b, i, 0))
    col = lambda r: pl.BlockSpec((None, r, tm), lambda b, i: (b, 0, i))
    full = lambda a: pl.BlockSpec(a.shape, lambda b, i: (0, 0))
    tshape = lambda r, dt: jax.ShapeDtypeStruct((n, r, t), dt)
    return pl.pallas_call(
        _l1_in_prompt_kernel,
        grid=(n, t // tm),
        in_specs=[row(d), full(nw), full(kvnw), full(wq), full(wg), full(wkv)],
        out_specs=[row(nq), row(ng), pl.BlockSpec((None, tm // CMP_STRIDE, CHUNK_LANES), lambda b, i: (b, i, 0)),
                   col(KV_ROW), col(KV_ROW), col(KV_ROW), col(2 * KV_ROW)],
        out_shape=[jax.ShapeDtypeStruct((n, t, nq), BF16), jax.ShapeDtypeStruct((n, t, ng), F32),
                   jax.ShapeDtypeStruct((n, t // CMP_STRIDE, CHUNK_LANES), F32),
                   tshape(KV_ROW, F32), tshape(KV_ROW, F32), tshape(KV_ROW, F32), tshape(2 * KV_ROW, BF16)],
        scratch_shapes=[pltpu.VMEM((KV_ROW // LANES, tm, LANES), F32)],
        compiler_params=_params("arbitrary", "arbitrary"),
        name="l1_in_prompt",
    )(h, nw, kvnw, wq, wg, wkv)


def _l1_in_sample(h, nw, kvnw, wq, wg, wkv):
    m, d = h.shape
    tm = ROW_TILE
    nq, ng = wq.shape[1], wg.shape[1]
    row = lambda w: pl.BlockSpec((tm, w), lambda i: (i, 0))
    full = lambda a: pl.BlockSpec(a.shape, lambda i: (0, 0))
    return pl.pallas_call(
        _l1_in_sample_kernel,
        grid=(m // tm,),
        in_specs=[row(d), full(nw), full(kvnw), full(wq), full(wg), full(wkv)],
        out_specs=[row(nq), row(ng), row(KV_ROW), row(KV_ROW), row(KV_ROW)],
        out_shape=[jax.ShapeDtypeStruct((m, nq), F32), jax.ShapeDtypeStruct((m, ng), F32),
                   jax.ShapeDtypeStruct((m, KV_ROW), F32), jax.ShapeDtypeStruct((m, KV_ROW), F32),
                   jax.ShapeDtypeStruct((m, KV_ROW), F32)],
        compiler_params=_params("arbitrary"),
        name="l1_in_sample",
    )(h, nw, kvnw, wq, wg, wkv)


def _compress_tokens(gather, tc, pe_ref, w1d_ref, w1p_ref, w2_ref, shift_ref):
    pieces = []
    for p in range(KV_ROW // LANES):
        k = p // 2
        xp = jnp.concatenate([gather(p, j) for j in range(CMP_STRIDE)], axis=1).astype(BF16)
        part = _dot(xp, w1p_ref[k])
        shift_ref[...] = part[:, LANES:2 * LANES]
        peb = _dot(pe_ref[k].astype(BF16), w1d_ref[k])
        pre = part[0:tc, 0:LANES] + shift_ref[pl.ds(1, tc), :] + peb
        pieces.append(_dot(_silu(pre).astype(BF16), w2_ref[k]))
    return jnp.concatenate(pieces, axis=1)


def _compress_prompt_kernel(x_ref, halo_ref, pe_ref, w1d_ref, w1p_ref, w2_ref, kc_ref, kct_ref, xs_ref, shift_ref):
    tc = x_ref.shape[0]
    xs_ref[0:tc, :] = x_ref[...]
    xs_ref[tc:tc + SUBLANES, :] = halo_ref[...]
    gather = lambda p, j: xs_ref[:, (j * 4 + p) * LANES:(j * 4 + p + 1) * LANES]
    kc = _compress_tokens(gather, tc, pe_ref, w1d_ref, w1p_ref, w2_ref, shift_ref)
    kc_ref[...] = kc.astype(kc_ref.dtype)
    kct_ref[...] = kc.T.astype(kct_ref.dtype)


def _compress_prompt(chunks, pe, w1d, w1p, w2):
    n, c, _ = chunks.shape
    tc = CMP_TILE
    nhalo = c // SUBLANES
    full = lambda a: pl.BlockSpec(a.shape, lambda b, i: (0,) * a.ndim)
    return pl.pallas_call(
        _compress_prompt_kernel,
        grid=(n, c // tc),
        in_specs=[
            pl.BlockSpec((None, tc, CHUNK_LANES), lambda b, i: (b, i, 0)),
            pl.BlockSpec((None, SUBLANES, CHUNK_LANES),
                         lambda b, i: (b, jnp.minimum((i + 1) * (tc // SUBLANES), nhalo - 1), 0)),
            full(pe), full(w1d), full(w1p), full(w2),
        ],
        out_specs=[pl.BlockSpec((None, tc, KV_ROW), lambda b, i: (b, i, 0)),
                   pl.BlockSpec((None, KV_ROW, tc), lambda b, i: (b, 0, i))],
        out_shape=[jax.ShapeDtypeStruct((n, c, KV_ROW), BF16), jax.ShapeDtypeStruct((n, KV_ROW, c), BF16)],
        scratch_shapes=[pltpu.VMEM((tc + SUBLANES, CHUNK_LANES), F32), pltpu.VMEM((tc + SUBLANES, LANES), F32)],
        compiler_params=_params("arbitrary", "arbitrary"),
        name="compress_prompt",
    )(chunks, chunks, pe, w1d, w1p, w2)


def _compress_sample_kernel(pt_ref, new_ref, pe_ref, w1d_ref, w1p_ref, w2_ref, *rest, n_pages, bps):
    page_refs = rest[:bps * n_pages]
    kc_ref, xs_ref, shift_ref = rest[bps * n_pages:]
    for bi in range(bps):
        _compress_sample_one(new_ref.at[bi], pe_ref, w1d_ref, w1p_ref, w2_ref,
                             page_refs[bi * n_pages:(bi + 1) * n_pages], kc_ref.at[bi], xs_ref.at[bi], shift_ref.at[bi])


def _compress_sample_one(new_ref, pe_ref, w1d_ref, w1p_ref, w2_ref, page_refs, kc_ref, xs_ref, shift_ref):
    n_pages = len(page_refs)
    n_pieces = KV_ROW // LANES
    past = n_pages * PAGE_SIZE
    seq = new_ref.shape[0]
    tc = past // CMP_STRIDE
    per_page = PAGE_SIZE // CMP_STRIDE
    pitch = tc + SUBLANES
    for p in range(n_pieces):
        for j in range(CMP_STRIDE):
            xs_ref[p, j * pitch + tc:(j + 1) * pitch, :] = jnp.zeros((SUBLANES, LANES), F32)
        xs_ref[p, pl.ds(tc, seq, stride=pitch), :] = new_ref[:, p * LANES:(p + 1) * LANES]
    for pg in range(n_pages):
        page = page_refs[pg][...]
        for p in range(n_pieces):
            tile = page[p * LANES:(p + 1) * LANES, :].T
            for v in range(PAGE_SIZE // SUBLANES):
                c, j0 = divmod(v * SUBLANES, CMP_STRIDE)
                xs_ref[p, pl.ds(j0 * pitch + pg * per_page + c, SUBLANES, stride=pitch), :] = (
                    tile[v * SUBLANES:(v + 1) * SUBLANES, :])
    gather = lambda p, j: xs_ref[p, j * pitch:(j + 1) * pitch, :]
    kc = _compress_tokens(gather, tc, pe_ref, w1d_ref, w1p_ref, w2_ref, shift_ref)
    kc_ref[...] = kc.astype(kc_ref.dtype)


def _compress_sample(page_table, cache_t, new_rows, pe, w1d, w1p, w2):
    nb, seq, _ = new_rows.shape
    n_pages = page_table.shape[0] // nb
    bps = SAMPLE_BATCH_PER_STEP
    tc = n_pages * PAGE_SIZE // CMP_STRIDE
    full = lambda a: pl.BlockSpec(a.shape, lambda b, pt: (0,) * a.ndim)
    page_specs = [pl.BlockSpec((None, KV_ROW, PAGE_SIZE),
                               lambda b, pt, bi=bi, p=p: (pt[(b * bps + bi) * n_pages + p], 0, 0))
                  for bi in range(bps) for p in range(n_pages)]
    return pl.pallas_call(
        functools.partial(_compress_sample_kernel, n_pages=n_pages, bps=bps),
        grid_spec=pltpu.PrefetchScalarGridSpec(
            num_scalar_prefetch=1,
            grid=(nb // bps,),
            in_specs=[pl.BlockSpec((bps, seq, KV_ROW), lambda b, pt: (b, 0, 0)),
                      full(pe), full(w1d), full(w1p), full(w2)] + page_specs,
            out_specs=pl.BlockSpec((bps, tc, KV_ROW), lambda b, pt: (b, 0, 0)),
            scratch_shapes=[pltpu.VMEM((bps, KV_ROW // LANES, (tc + SUBLANES) * CMP_STRIDE, LANES), F32),
                            pltpu.VMEM((bps, tc + SUBLANES, LANES), F32)],
        ),
        out_shape=jax.ShapeDtypeStruct((nb, tc, KV_ROW), BF16),
        compiler_params=_params("arbitrary"),
        name="compress_sample",
    )(page_table, new_rows, pe, w1d, w1p, w2, *([cache_t] * (bps * n_pages)))


def _top_n_rows(score, n_pick):
    rows = lax.broadcasted_iota(jnp.int32, score.shape, 0).astype(F32)
    sel = jnp.zeros(score.shape, F32)
    s = score
    for _ in range(n_pick):
        m = jnp.max(s, axis=0, keepdims=True)
        first = jnp.min(jnp.where(s == m, rows, float(score.shape[0])), axis=0, keepdims=True)
        hit = rows == first
        sel = jnp.where(hit, 1.0, sel)
        s = jnp.where(hit, -jnp.inf, s)
    return sel


def _softmax_cols(s, mask):
    s = jnp.where(mask, s, NEG)
    m = jnp.max(s, axis=0, keepdims=True)
    e = jnp.where(mask, jnp.exp2(s - m), 0.0)
    return e * (1.0 / jnp.maximum(jnp.sum(e, axis=0, keepdims=True), 1e-30))


def _softmax_rows_unnorm(s, mask):
    s = jnp.where(mask, s, NEG)
    m = jnp.max(s, axis=1, keepdims=True)
    e = jnp.where(mask, jnp.exp2(s - m), 0.0)
    return e, 1.0 / jnp.maximum(jnp.sum(e, axis=1, keepdims=True), 1e-30)


def _edge_biases(tq):
    t = lax.broadcasted_iota(jnp.int32, (tq, tq), 0)
    i = lax.broadcasted_iota(jnp.int32, (tq, tq), 1)
    return jnp.where(i <= t, 0.0, NEG), jnp.where(i >= t, 0.0, NEG)


def _block_bias(d, causal, lower, lower_at):
    bias = jnp.where(d == 0, causal, jnp.where(d < 0, NEG, 0.0))
    if lower_at is not None:
        bias = jnp.where(d == lower_at, lower, bias)
    return bias


def _add_per_head(s, bias, n_heads):
    tq = bias.shape[0]
    return jnp.concatenate([s[h * tq:(h + 1) * tq] + bias for h in range(n_heads)], axis=0)


def _block_scores(imp, pos, n_sel):
    jj = lax.broadcasted_iota(jnp.int32, imp.shape, 0)
    valid = (jj * SEL_BLK <= pos) & (jj < n_sel)
    cur = pos // SEL_BLK
    forced = (jj == 0) | (jj == cur) | (jj == cur - 1)
    score = jnp.where(valid & forced, BIG, jnp.where(valid, imp, -BIG))
    return jnp.where(jj < n_sel, score, -jnp.inf)


def _split_dot(w_bf16, x):
    hi = x.astype(BF16)
    lo = (x - hi.astype(F32)).astype(BF16)
    return _dot(w_bf16, hi) + _dot(w_bf16, lo)


def _with_ones(v_half, ones, first):
    return jnp.concatenate([v_half, ones] if first else [ones, v_half], axis=0)


def _norm_by_other_half(acc):
    return acc * (1.0 / jnp.maximum(pltpu.roll(acc, HEAD_DIM, axis=1), 1e-30))


def _attn_prompt_kernel(q_ref, g_ref, kc_ref, vct_ref, skt_ref, svt_ref, wkt_ref, wvt_ref, et_ref, aggt_ref, z_ref):
    tq = q_ref.shape[0]
    tkc = KEY_CHUNK
    qt = pl.program_id(2)
    t0 = qt * tq
    heads = 2 * GROUP
    r_rows = heads * tq
    n_cmp = kc_ref.shape[0]
    n_sel = et_ref.shape[1] // SEL_BLK
    half = lax.broadcasted_iota(jnp.int32, (tq, LANES), 1) // HEAD_DIM
    qpos_row = t0 + (lax.broadcasted_iota(jnp.int32, (1, r_rows), 1) & (tq - 1))

    qs = [q_ref[:, h * LANES:(h + 1) * LANES] for h in range(heads)]
    qall = jnp.concatenate(qs, axis=0)

    causal_b, lower_b = _edge_biases(tq)

    def pv_ones(p, v, ones_rows):
        return jnp.concatenate(
            [_dot_nt(p[gg * GROUP * tq:(gg + 1) * GROUP * tq],
                     _with_ones(v[gg * HEAD_DIM:(gg + 1) * HEAD_DIM], ones_rows, gg == 0)) for gg in range(2)], axis=0)

    cend = lax.broadcasted_iota(jnp.int32, (n_cmp, 1), 0) * CMP_STRIDE + (CMP_BLK - 1)
    pt = _softmax_cols(_dot_nt(kc_ref[...], qall), cend <= qpos_row)
    psum = []
    for gg in range(2):
        acc = pt[:, gg * GROUP * tq:(gg * GROUP + 1) * tq]
        for r in range(1, GROUP):
            acc = acc + pt[:, (gg * GROUP + r) * tq:(gg * GROUP + r + 1) * tq]
        psum.append(acc)
    imp = _split_dot(aggt_ref[...], jnp.concatenate(psum, axis=1))
    oct = _dot(vct_ref[...], pt.astype(BF16))

    n_wblk = WINDOW // tq + 1
    start = pl.multiple_of(jnp.maximum(t0 - WINDOW, 0), LANES)
    back0 = jnp.minimum(WINDOW // tq, qt)
    sw = jnp.concatenate(
        [_add_per_head(_dot(qall, wkt_ref[:, pl.ds(pl.multiple_of(start + b * tq, LANES), tq)]),
                       _block_bias(back0 - b, causal_b, lower_b, WINDOW // tq), heads) for b in range(n_wblk)],
        axis=1).astype(BF16)
    ew = jnp.exp2(sw - jnp.max(sw, axis=1, keepdims=True))
    o_w = _norm_by_other_half(pv_ones(ew, wvt_ref[:, pl.ds(start, n_wblk * tq)], jnp.ones((HEAD_DIM, n_wblk * tq), BF16)))

    gates = g_ref[...]
    part = [gates[:, 3 * h:3 * h + 1] * oct[:, h * tq:(h + 1) * tq].T
            + gates[:, 3 * h + 2:3 * h + 3] * o_w[h * tq:(h + 1) * tq, :] for h in range(heads)]

    pos = t0 + (lax.broadcasted_iota(jnp.int32, imp.shape, 1) & (tq - 1))
    selt = _top_n_rows(_block_scores(imp, pos, n_sel), min(TOP_N, n_sel))
    selm1 = [(selt[:, gg * tq:(gg + 1) * tq].T - 1.0).astype(BF16) for gg in range(2)]
    qaug = jnp.concatenate([jnp.concatenate([qs[h], selm1[h // GROUP]], axis=1) for h in range(heads)], axis=0)

    def sel_chunk(start, width, carry, diag_block):
        m_i, acc = carry
        keys = pl.ds(start, width)
        rhs = jnp.concatenate([skt_ref[:, keys], et_ref[:, keys]], axis=0)
        s = _dot(qaug, rhs)
        if diag_block is not None:
            s = jnp.concatenate(
                [_add_per_head(s[:, b * tq:(b + 1) * tq], _block_bias(diag_block - b, causal_b, lower_b, None), heads)
                 for b in range(width // tq)], axis=1)
        sb = s.astype(BF16)
        m_new = jnp.maximum(m_i, jnp.max(sb, axis=1, keepdims=True).astype(F32))
        alpha = jnp.exp2(m_i - m_new)
        p = jnp.exp2(sb - m_new.astype(BF16))
        return m_new, alpha * acc + pv_ones(p, svt_ref[:, keys], jnp.ones((HEAD_DIM, width), BF16))

    n_full = t0 // tkc
    n_wide = n_full // 2
    aligned = lambda c, width: pl.multiple_of(c * width, width)
    init = (jnp.full((r_rows, 1), M_INIT, F32), jnp.zeros((r_rows, LANES), F32))
    carry = sel_chunk(aligned(n_full, tkc), tkc, init, qt - n_full * (tkc // tq))
    carry = lax.fori_loop(0, n_wide, lambda c, cr: sel_chunk(aligned(c, 2 * tkc), 2 * tkc, cr, None), carry)
    _, acc_s = lax.fori_loop(2 * n_wide, n_full, lambda c, cr: sel_chunk(aligned(c, tkc), tkc, cr, None), carry)
    o_s = _norm_by_other_half(acc_s)

    head_out = [part[h] + gates[:, 3 * h + 1:3 * h + 2] * o_s[h * tq:(h + 1) * tq, :] for h in range(heads)]
    for i in range(heads // 2):
        gg = (2 * i) // GROUP
        a, b = head_out[2 * i], head_out[2 * i + 1]
        packed = (jnp.where(half == 0, a, pltpu.roll(b, HEAD_DIM, axis=1)) if gg == 0
                  else jnp.where(half == 0, pltpu.roll(a, HEAD_DIM, axis=1), b))
        z_ref[:, i * LANES:(i + 1) * LANES] = packed.astype(z_ref.dtype)


def _attn_prompt(q2, gates2, kvc, kvct, kvtb, e_t, aggt):
    n, t, _ = q2.shape
    tq = Q_TILE
    npair = N_KV_HEADS // 2
    pair_w = 2 * GROUP * LANES
    kv_slab = lambda branch, kv: pl.BlockSpec(
        (None, LANES, t), lambda b, gp, i: (b, branch * 2 * npair + kv * npair + gp, 0))
    full = lambda a: pl.BlockSpec(a.shape, lambda b, gp, i: (0,) * a.ndim)
    return pl.pallas_call(
        _attn_prompt_kernel,
        grid=(n, npair, t // tq),
        in_specs=[
            pl.BlockSpec((None, tq, pair_w), lambda b, gp, i: (b, i, gp)),
            pl.BlockSpec((None, tq, LANES), lambda b, gp, i: (b, i, gp)),
            pl.BlockSpec((None, kvc.shape[1], LANES), lambda b, gp, i: (b, 0, gp)),
            pl.BlockSpec((None, LANES, kvct.shape[2]), lambda b, gp, i: (b, npair + gp, 0)),
            kv_slab(0, 0), kv_slab(0, 1), kv_slab(1, 0), kv_slab(1, 1),
            full(e_t), full(aggt),
        ],
        out_specs=pl.BlockSpec((None, tq, 2 * GROUP * HEAD_DIM), lambda b, gp, i: (b, i, gp)),
        out_shape=jax.ShapeDtypeStruct((n, t, N_HEADS * HEAD_DIM), BF16),
        compiler_params=_params("arbitrary", "arbitrary", "arbitrary"),
        name="attn_prompt",
    )(q2, gates2, kvc, kvct, kvtb, kvtb, kvtb, kvtb, e_t, aggt)


def _attn_sample_kernel(pt_ref, qbd_ref, g_ref, kvc_ref, nslc_ref, wst_ref, nwin_ref, et_ref, enew_ref, aggt_ref,
                        *rest, n_pages, past_len, bps):
    page_refs = rest[:bps * n_pages]
    z_ref, wout_ref = rest[bps * n_pages:]
    seq = qbd_ref.shape[1]
    rows = GROUP * N_KV_HEADS * seq
    n_sel = -(-(n_pages * PAGE_SIZE + seq) // SEL_BLK)
    front = [_attn_sample_front(qbd_ref.at[bi], kvc_ref.at[bi], aggt_ref, past_len, n_sel) for bi in range(bps)]
    selt = _top_n_rows(jnp.concatenate([f[2] for f in front], axis=1), min(TOP_N, n_sel))
    for bi in range(bps):
        _attn_sample_back(front[bi][0], front[bi][1], selt[:, bi * rows:(bi + 1) * rows], g_ref.at[bi],
                          nslc_ref.at[bi], wst_ref.at[bi], nwin_ref.at[bi], et_ref, enew_ref,
                          page_refs[bi * n_pages:(bi + 1) * n_pages], z_ref.at[bi], wout_ref.at[bi], past_len)


def _attn_sample_front(qbd_ref, kvc_ref, aggt_ref, past_len, n_sel):
    seq = qbd_ref.shape[0]
    kw = KV_HALF
    rows = GROUP * N_KV_HEADS * seq
    n_cmp = kvc_ref.shape[0]

    qbd = jnp.concatenate(
        [qbd_ref[:, (g * GROUP + r) * kw:(g * GROUP + r + 1) * kw] for r in range(GROUP) for g in range(N_KV_HEADS)],
        axis=0).astype(BF16)
    qpos_row = past_len + (lax.broadcasted_iota(jnp.int32, (1, rows), 1) & (seq - 1))

    cend = lax.broadcasted_iota(jnp.int32, (n_cmp, 1), 0) * CMP_STRIDE + (CMP_BLK - 1)
    pt = _softmax_cols(_dot_nt(kvc_ref[:, 0:kw], qbd), cend <= qpos_row)
    imp = _split_dot(aggt_ref[...], pt)
    per_r = N_KV_HEADS * seq
    imp_g = imp
    for r in range(1, GROUP):
        imp_g = imp_g + pltpu.roll(imp, r * per_r, axis=1)
    vct = kvc_ref[:, kw:2 * kw].astype(F32).T.astype(BF16)
    o_c = _dot(vct, pt.astype(BF16)).T
    return qbd, o_c, _block_scores(imp_g, jnp.broadcast_to(qpos_row, imp_g.shape), n_sel)


def _attn_sample_back(qbd, o_c, selt, g_ref, nslc_ref, wst_ref, nwin_ref, et_ref, enew_ref, page_refs,
                      z_ref, wout_ref, past_len):
    n_pages = len(page_refs)
    seq = g_ref.shape[0]
    kw = KV_HALF
    rows = GROUP * N_KV_HEADS * seq
    w_len = wst_ref.shape[1]
    qpos_col = past_len + (lax.broadcasted_iota(jnp.int32, (rows, 1), 0) & (seq - 1))
    selm1 = (selt.T - 1.0).astype(BF16)
    qaug = jnp.concatenate([qbd, selm1], axis=1)

    pad = jnp.zeros((PAGE_SIZE - seq, kw), F32)
    new_tile = lambda ref, lo: jnp.concatenate([ref[:, lo:lo + kw], pad], axis=0).astype(BF16)

    s_tiles = []
    for p in range(n_pages):
        rhs = jnp.concatenate([page_refs[p][0:kw, :].astype(BF16), et_ref[:, p * PAGE_SIZE:(p + 1) * PAGE_SIZE]], axis=0)
        s_tiles.append(_dot(qaug, rhs))
    s_tiles.append(_dot_nt(qaug, jnp.concatenate([new_tile(nslc_ref, 0), enew_ref[...]], axis=1)))
    s_all = jnp.concatenate(s_tiles, axis=1)
    kpos = lax.broadcasted_iota(jnp.int32, (1, s_all.shape[1]), 1)
    es, inv_s = _softmax_rows_unnorm(s_all, kpos <= qpos_col)
    es = es.astype(BF16)
    o_s = _dot(es[:, n_pages * PAGE_SIZE:], new_tile(nslc_ref, kw))
    for p in range(n_pages):
        o_s = o_s + _dot_nt(es[:, p * PAGE_SIZE:(p + 1) * PAGE_SIZE], page_refs[p][kw:2 * kw, :].astype(BF16))
    o_s = o_s * inv_s

    wst = wst_ref[...]
    sw = jnp.concatenate([_dot(qbd, wst[0:kw, :].astype(BF16)), _dot_nt(qbd, new_tile(nwin_ref, 0))], axis=1)
    kpos_w = past_len - w_len + lax.broadcasted_iota(jnp.int32, (1, w_len + PAGE_SIZE), 1)
    mw = (kpos_w <= qpos_col) & (kpos_w >= qpos_col - WINDOW) & (kpos_w >= 0)
    ew, inv_w = _softmax_rows_unnorm(sw, mw)
    ew = ew.astype(BF16)
    o_w = (_dot_nt(ew[:, 0:w_len], wst[kw:2 * kw, :].astype(BF16)) + _dot(ew[:, w_len:], new_tile(nwin_ref, kw))) * inv_w

    gates = g_ref[...]
    half = lax.broadcasted_iota(jnp.int32, (seq, LANES), 1) // HEAD_DIM

    def head_out(g, r):
        rs = slice((r * N_KV_HEADS + g) * seq, (r * N_KV_HEADS + g + 1) * seq)
        cs = slice((g // 2) * LANES, (g // 2 + 1) * LANES)
        c = (g * GROUP + r) * 3
        return (gates[:, c:c + 1] * o_c[rs, cs] + gates[:, c + 1:c + 2] * o_s[rs, cs]
                + gates[:, c + 2:c + 3] * o_w[rs, cs])

    for g in range(N_KV_HEADS):
        for i in range(GROUP // 2):
            a, b = head_out(g, 2 * i), head_out(g, 2 * i + 1)
            packed = (jnp.where(half == 0, a, pltpu.roll(b, HEAD_DIM, axis=1)) if g % 2 == 0
                      else jnp.where(half == 0, pltpu.roll(a, HEAD_DIM, axis=1), b))
            col = g * (GROUP // 2) + i
            z_ref[:, col * LANES:(col + 1) * LANES] = packed

    new_t = jnp.concatenate([nwin_ref[...], jnp.zeros((LANES - seq, KV_ROW), F32)], axis=0).T
    shifted = pltpu.roll(wst, w_len - seq, axis=1)
    lane = lax.broadcasted_iota(jnp.int32, (KV_ROW, LANES), 1)
    wout_ref[:, 0:w_len - LANES] = shifted[:, 0:w_len - LANES]
    wout_ref[:, w_len - LANES:w_len] = jnp.where(lane >= LANES - seq, pltpu.roll(new_t, LANES - seq, axis=1),
                                                 shifted[:, w_len - LANES:w_len])


def _attn_sample(page_table, qbd, gates, kvc, new_slc, cache_t, win_t, new_win, e_t, e_new, aggt, past_len):
    nb, seq, _ = qbd.shape
    n_pages = page_table.shape[0] // nb
    w_len = win_t.shape[2]
    assert w_len == min(WINDOW, past_len + seq) and w_len % LANES == 0
    bps = SAMPLE_BATCH_PER_STEP
    full = lambda a: pl.BlockSpec(a.shape, lambda b, pt: (0,) * a.ndim)
    per_b = lambda a: pl.BlockSpec((bps,) + a.shape[1:], lambda b, pt: (b,) + (0,) * (a.ndim - 1))
    page_specs = [pl.BlockSpec((None, KV_ROW, PAGE_SIZE),
                               lambda b, pt, bi=bi, p=p: (pt[(b * bps + bi) * n_pages + p], 0, 0))
                  for bi in range(bps) for p in range(n_pages)]
    zw = N_HEADS * HEAD_DIM
    return pl.pallas_call(
        functools.partial(_attn_sample_kernel, n_pages=n_pages, past_len=past_len, bps=bps),
        grid_spec=pltpu.PrefetchScalarGridSpec(
            num_scalar_prefetch=1,
            grid=(nb // bps,),
            in_specs=[per_b(qbd), per_b(gates), per_b(kvc), per_b(new_slc), per_b(win_t), per_b(new_win),
                      full(e_t), full(e_new), full(aggt)] + page_specs,
            out_specs=[pl.BlockSpec((bps, seq, zw), lambda b, pt: (b, 0, 0)),
                       pl.BlockSpec((bps, KV_ROW, w_len), lambda b, pt: (b, 0, 0))],
        ),
        out_shape=[jax.ShapeDtypeStruct((nb, seq, zw), F32), jax.ShapeDtypeStruct((nb, KV_ROW, w_len), F32)],
        compiler_params=_params("arbitrary"),
        name="attn_sample",
    )(page_table, qbd, gates, kvc, new_slc, win_t, new_win, e_t, e_new, aggt, *([cache_t] * (bps * n_pages)))


def _pad_cols(w, width):
    return jnp.pad(w, ((0, 0), (0, width - w.shape[1])))


def _pack_weights(b_in_w, cmp_w1, cmp_w2):
    d = b_in_w.shape[0]
    nq = N_HEADS * HEAD_DIM
    wq = b_in_w[:, :nq].reshape(d, N_KV_HEADS, GROUP, HEAD_DIM)
    wgate = b_in_w[:, nq:]
    eye2 = jnp.eye(2, dtype=F32)
    eye4 = jnp.eye(N_KV_HEADS, dtype=F32)
    par = jnp.arange(N_KV_HEADS) % 2
    sel2 = jax.nn.one_hot(par, 2, dtype=F32)
    wq_p = jnp.einsum('dgrh,ga->dgrah', wq, sel2).reshape(d, N_HEADS * LANES)
    wq_s = jnp.einsum('dgrh,ga->dgrah', wq, eye4).reshape(d, N_HEADS * N_KV_HEADS * HEAD_DIM)
    per_pair = 2 * GROUP * 3
    wg_p = jnp.concatenate([_pad_cols(wgate[:, i * per_pair:(i + 1) * per_pair], LANES)
                            for i in range(N_KV_HEADS // 2)], axis=1)
    wg_s = _pad_cols(wgate, LANES)
    w1 = cmp_w1.reshape(2, CMP_BLK // CMP_STRIDE, CMP_STRIDE, HEAD_DIM, -1)
    hid = w1.shape[-1]
    w1p = jnp.einsum('krjdh,ab->kjadrbh', w1, eye2).reshape(2, CMP_STRIDE * LANES, 2 * 2 * hid)
    w1d = jnp.concatenate([cmp_w1.reshape(2, CMP_BLK * HEAD_DIM, hid)] * 2, axis=2)
    w2p = jnp.einsum('khe,ab->kahbe', cmp_w2, eye2).reshape(2, 2 * hid, 2 * HEAD_DIM)
    cast = lambda a: a.astype(BF16)
    return tuple(map(cast, (wq_p, wq_s, wg_p, wg_s, w1p, w1d, w2p)))


def _agg_t(n_cmp, n_sel_pad):
    c0 = jnp.arange(n_cmp) * CMP_STRIDE
    s0 = jnp.arange(n_sel_pad) * SEL_BLK
    ov = jnp.clip(jnp.minimum(c0[None, :] + CMP_BLK, s0[:, None] + SEL_BLK) - jnp.maximum(c0[None, :], s0[:, None]),
                  0, None)
    return (ov.astype(F32) / CMP_STRIDE).astype(BF16)


def _block_onehot(key0, n_keys):
    blk = (key0 + jnp.arange(n_keys)) // SEL_BLK
    return (jax.nn.one_hot(blk, LANES, dtype=F32) * BIG).astype(BF16)


def _feature_major(a):
    return a.transpose(0, 2, 3, 4, 1).reshape(a.shape[0], KV_ROW, a.shape[1])


def _position_major(a_t):
    b, _, pos = a_t.shape
    return a_t.reshape(b, 2, N_KV_HEADS, HEAD_DIM, pos).transpose(0, 4, 1, 2, 3)


def kernel(x_prompt, x_sample, cache_cmp_kv, cache_slc_kv, state_win_kv, state_conv, page_table, norm_w, final_norm_w, a_in_w, a_conv_w, a_out_w, b_in_w, b_out_w, kv_norm_w, kv_w, cmp_pe, cmp_w1, cmp_w2, ffn_in_w, ffn_out_w):
    n, t, d = x_prompt.shape
    nb, ts, _ = x_sample.shape
    n_pages = page_table.shape[1]
    past_len = n_pages * PAGE_SIZE
    d_ff = ffn_out_w.shape[1]
    assert a_in_w.shape[0] == 1 and b_in_w.shape[0] == 1 and d == N_HEADS * HEAD_DIM
    assert t % (CMP_TILE * CMP_STRIDE) == 0 and t % KEY_CHUNK == 0 and KEY_CHUNK % Q_TILE == 0
    assert ts == SUBLANES and (nb * ts) % ROW_TILE == 0 and d_ff % (2 * LANES) == 0
    assert nb % SAMPLE_BATCH_PER_STEP == 0 and WINDOW % Q_TILE == 0

    wq_p, wq_s, wg_p, wg_s, w1p, w1d, w2p = _pack_weights(b_in_w[0], cmp_w1, cmp_w2)
    a_in, a_out, kvw = a_in_w[0].astype(BF16), a_out_w[0].astype(BF16), kv_w.astype(BF16)
    b_out = b_out_w[0].astype(BF16)
    ffn_in, ffn_out = ffn_in_w.astype(BF16), ffn_out_w.astype(BF16)
    nw = norm_w.reshape(norm_w.shape[0], 2, 1, d)
    fnw = final_norm_w.reshape(1, d)
    kvnw = kv_norm_w.reshape(1, d)
    pe = cmp_pe.reshape(2, 1, CMP_BLK * HEAD_DIM)
    mp, ms = n * t, nb * ts

    z0, tail_p = _l0_in_prompt(x_prompt, nw[0, 0], a_in, a_conv_w[0])
    h1 = _mix_ffn(x_prompt.reshape(mp, d), z0.reshape(mp, d), a_out, nw[0, 1], ffn_in, ffn_out, 0, fnw, final=False)
    q2, g2, cmp_chunks, cmp_t, slc_t, win_t, kvtb = _l1_in_prompt(h1.reshape(n, t, d), nw[1, 0], kvnw, wq_p, wg_p, kvw)
    n_chunks = t // CMP_STRIDE
    kvc, kvct = _compress_prompt(cmp_chunks, pe, w1d, w1p, w2p)
    z1 = _attn_prompt(q2, g2, kvc, kvct, kvtb, _block_onehot(0, t).T, _agg_t(n_chunks, LANES))
    y_p = _mix_ffn(h1, z1.reshape(mp, -1), b_out, nw[1, 1], ffn_in, ffn_out, 1, fnw, final=True)

    zero = jnp.zeros((nb, ts - 2, d), F32)
    pre1 = jnp.concatenate([state_conv[0, :, 1:2], jnp.zeros((nb, ts - 1, d), F32)], axis=1).reshape(ms, d)
    pre2 = jnp.concatenate([state_conv[0], zero], axis=1).reshape(ms, d)
    xs = x_sample.reshape(ms, d)
    z0s, v_s = _l0_in_sample(xs, pre1, pre2, nw[0, 0], a_in, a_conv_w[0], seq=ts)
    h1s = _mix_ffn(xs, z0s, a_out, nw[0, 1], ffn_in, ffn_out, 0, fnw, final=False)
    qbd, gs, cmp_s, slc_s, win_s_new = _l1_in_sample(h1s, nw[1, 0], kvnw, wq_s, wg_s, kvw)
    pt_flat = page_table.reshape(-1)
    kvc_s = _compress_sample(pt_flat, _feature_major(cache_cmp_kv), cmp_s.reshape(nb, ts, KV_ROW), pe, w1d, w1p, w2p)
    z1s, win_s_t = _attn_sample(pt_flat, qbd.reshape(nb, ts, -1), gs.reshape(nb, ts, -1), kvc_s,
                                slc_s.reshape(nb, ts, KV_ROW), _feature_major(cache_slc_kv),
                                _feature_major(state_win_kv), win_s_new.reshape(nb, ts, KV_ROW),
                                _block_onehot(0, past_len).T, _block_onehot(past_len, PAGE_SIZE),
                                _agg_t(kvc_s.shape[1], LANES), past_len)
    y_s = _mix_ffn(h1s, z1s.reshape(ms, -1), b_out, nw[1, 1], ffn_in, ffn_out, 1, fnw, final=True)

    kv_shape = (2, N_KV_HEADS, HEAD_DIM)
    keep_p = min(WINDOW, t)
    return (
        y_p.reshape(n, t, d),
        y_s.reshape(nb, ts, d),
        tail_p[:, SUBLANES - (CONV_W - 1):][None],
        v_s.reshape(nb, ts, d)[:, ts - (CONV_W - 1):][None],
        _position_major(cmp_t),
        cmp_s.reshape((nb, ts) + kv_shape),
        _position_major(slc_t),
        slc_s.reshape((nb, ts) + kv_shape),
        _position_major(win_t[:, :, t - keep_p:]),
        _position_major(win_s_t),
    )
```
